```python
import math
import jax, jax.numpy as jnp
from jax import lax
import numpy as np

D_MODEL = 2048
BATCH = 1
SEQ = 16384
DEPTH = 1
DEC_BATCH = 8
DEC_SEQ = 2048
PAST_LEN = 128

HEAD_DIM = 128
N_ATTN_HEADS = 8
ATTN_WIDTH = N_ATTN_HEADS * HEAD_DIM
LRU_WIDTH = 1024
N_LRU_BLOCKS = 8
LRU_BLOCK = LRU_WIDTH // N_LRU_BLOCKS
MIX_WIDTH = ATTN_WIDTH + LRU_WIDTH
IN_WIDTH = 3 * ATTN_WIDTH + 2 * LRU_WIDTH
D_FF = 3 * D_MODEL
DILATED_BRANCHES = ((128, 1), (512, 4), (2048, 16))
ROPE_THETA = 500000.0
ROPE_DIM = HEAD_DIM // 4
LRU_C = 8.0
LRU_CONV_WIDTH = 4
FF_CONV_WIDTH = 3
NORM_EPS = 1e-6
NEG_INF = -1e30

kernel_name = "hymba_dilated_attn_rglru_convffn_encoder"


def rmsnorm(x, g):
    xf = x.astype(jnp.float32)
    y = xf * lax.rsqrt(jnp.mean(xf * xf, axis=-1, keepdims=True) + NORM_EPS)
    return (y * g.astype(jnp.float32)).astype(x.dtype)


def partial_rope(t, pos):
    half = ROPE_DIM // 2
    inv = ROPE_THETA ** (-jnp.arange(half, dtype=jnp.float32) / half)
    ang = pos.astype(jnp.float32)[:, None] * inv[None, :]
    cos = jnp.cos(ang)[None, :, None, :]
    sin = jnp.sin(ang)[None, :, None, :]
    tf = t.astype(jnp.float32)
    t1 = tf[..., :half]
    t2 = tf[..., half:ROPE_DIM]
    out = jnp.concatenate([t1 * cos - t2 * sin, t1 * sin + t2 * cos, tf[..., ROPE_DIM:]], axis=-1)
    return out.astype(t.dtype)


def centred_dwconv(x, w, b, left):
    K = w.shape[0]
    S = x.shape[1]
    xp = jnp.pad(x, ((0, 0), (left, K - 1 - left), (0, 0)))
    return sum((xp[:, j:j + S] * w[j] for j in range(K)), b)


def block_neighbours(t, axis):
    n = t.shape[axis]
    zero = jnp.zeros_like(lax.slice_in_dim(t, 0, 1, axis=axis))
    prev = jnp.concatenate([zero, lax.slice_in_dim(t, 0, n - 1, axis=axis)], axis=axis)
    nxt = jnp.concatenate([lax.slice_in_dim(t, 1, n, axis=axis), zero], axis=axis)
    return jnp.concatenate([prev, t, nxt], axis=axis + 1)


def dilated_branch(q, k, v, window, dil):
    B, S, H, Dh = q.shape
    nk = window // 2 // dil
    unit = dil * nk
    S_pad = -(-S // unit) * unit
    L = S_pad // dil
    nb = L // nk
    pad = ((0, 0), (0, S_pad - S), (0, 0), (0, 0))

    def to_sub(t):
        t = jnp.pad(t, pad).reshape(B, L, dil, H, Dh).transpose(0, 2, 1, 3, 4)
        return t.reshape(B, dil, nb, nk, H, Dh)

    qs = to_sub(q)
    kw = block_neighbours(to_sub(k), 2)
    vw = block_neighbours(to_sub(v), 2)
    valid = (jnp.arange(S_pad) < S).reshape(L, dil).T.reshape(dil, nb, nk)
    kvalid = block_neighbours(valid, 1)
    rel = jnp.arange(3 * nk)[None, :] - nk - jnp.arange(nk)[:, None]
    mask = (jnp.abs(rel) <= nk)[None, None] & kvalid[:, :, None, :]

    s = jnp.einsum('brnqhd,brnkhd->brnhqk', qs, kw, preferred_element_type=jnp.float32) * (Dh ** -0.5)
    s = jnp.where(mask[None, :, :, None], s, NEG_INF)
    m = jnp.max(s, axis=-1)
    p = jnp.exp(s - m[..., None])
    den = jnp.sum(p, axis=-1)
    num = jnp.einsum('brnhqk,brnkhd->brnqhd', p, vw.astype(jnp.float32))

    def from_sub(t):
        X = t.shape[-1]
        t = t.reshape(B, dil, L, H, X).transpose(0, 2, 1, 3, 4).reshape(B, S_pad, H, X)
        return t[:, :S]

    m = from_sub(jnp.swapaxes(m, -1, -2)[..., None])
    den = from_sub(jnp.swapaxes(den, -1, -2)[..., None])
    return from_sub(num), den, m


def dilated_attention(q, k, v):
    outs = [dilated_branch(q, k, v, w, d) for (w, d) in DILATED_BRANCHES]
    nums = jnp.stack([o[0] for o in outs])
    dens = jnp.stack([o[1] for o in outs])
    ms = jnp.stack([o[2] for o in outs])
    scale = jnp.exp(ms - jnp.max(ms, axis=0, keepdims=True))
    return jnp.sum(scale * nums, axis=0) / jnp.sum(scale * dens, axis=0)


def rglru_mixer(xr, gate, conv_w, conv_b, w_a, b_a, w_x, b_x, lam):
    B, S, _ = xr.shape
    f32 = jnp.float32
    u = centred_dwconv(xr, conv_w, conv_b, LRU_CONV_WIDTH // 2).astype(f32)
    ub = u.reshape(B, S, N_LRU_BLOCKS, LRU_BLOCK)
    r = jax.nn.sigmoid(jnp.einsum('bsgi,zgij->zbsgj', ub, w_a.astype(f32)).reshape(2, B, S, LRU_WIDTH)
                       + b_a.astype(f32)[:, None, None, :])
    i = jax.nn.sigmoid(jnp.einsum('bsgi,zgij->zbsgj', ub, w_x.astype(f32)).reshape(2, B, S, LRU_WIDTH)
                       + b_x.astype(f32)[:, None, None, :])
    log_a = -LRU_C * jax.nn.softplus(-lam.astype(f32))[:, None, None, :] * r
    a = jnp.exp(log_a)
    inp = jnp.sqrt(-jnp.expm1(2.0 * log_a)) * (i * u[None])
    a = jnp.stack([a[0], jnp.flip(a[1], axis=1)])
    inp = jnp.stack([inp[0], jnp.flip(inp[1], axis=1)])

    def combine(e1, e2):
        a1, b1 = e1
        a2, b2 = e2
        return a1 * a2, a2 * b1 + b2

    _, h = lax.associative_scan(combine, (a, inp), axis=2)
    h = h[0] + jnp.flip(h[1], axis=1)
    return (h * jax.nn.gelu(gate.astype(f32))).astype(xr.dtype)


def conv_mlp(h, w_up, conv_w, conv_b, w_down):
    u = h @ w_up
    u = centred_dwconv(u, conv_w, conv_b, FF_CONV_WIDTH // 2)
    g, val = jnp.split(u, 2, axis=-1)
    return (jax.nn.gelu(g) * val) @ w_down


def encode(x, g_mix, w_in, w_out, g_attn_out, g_lru_out, conv_rg_w, conv_rg_b, rg_w_a, rg_b_a,
           rg_w_x, rg_b_x, rg_lam, g_mlp, w_up, conv_ff_w, conv_ff_b, w_down, g_final):
    B, S, _ = x.shape
    pos = jnp.arange(S)
    for l in range(DEPTH):
        h = rmsnorm(x, g_mix[l])
        z = h @ w_in[l]
        q, k, v, xr, gate = jnp.split(
            z, [ATTN_WIDTH, 2 * ATTN_WIDTH, 3 * ATTN_WIDTH, 3 * ATTN_WIDTH + LRU_WIDTH], axis=-1)
        q = partial_rope(q.reshape(B, S, N_ATTN_HEADS, HEAD_DIM), pos)
        k = partial_rope(k.reshape(B, S, N_ATTN_HEADS, HEAD_DIM), pos)
        v = v.reshape(B, S, N_ATTN_HEADS, HEAD_DIM)
        attn = dilated_attention(q, k, v).reshape(B, S, ATTN_WIDTH).astype(x.dtype)
        lru = rglru_mixer(xr, gate, conv_rg_w[l], conv_rg_b[l], rg_w_a[l], rg_b_a[l],
                          rg_w_x[l], rg_b_x[l], rg_lam[l])
        mixed = jnp.concatenate([rmsnorm(attn, g_attn_out[l]), rmsnorm(lru, g_lru_out[l])], axis=-1)
        x = x + mixed @ w_out[l]
        x = x + conv_mlp(rmsnorm(x, g_mlp[l]), w_up[l], conv_ff_w[l], conv_ff_b[l], w_down[l])
    return rmsnorm(x, g_final)


def setup_inputs(seed: int = 0) -> dict:
    key = jax.random.key(seed)
    ks = jax.random.split(key, 20)
    f32 = jnp.float32

    def nrm(k, shape, scale):
        return jax.random.normal(k, shape, f32) * scale

    u = jax.random.uniform(ks[12], (DEPTH, 2, LRU_WIDTH), f32, minval=0.9, maxval=0.999)
    a_base = u ** (1.0 / LRU_C)
    rg_lam = jnp.log(a_base) - jnp.log1p(-a_base)
    return {
        "x_prompt": nrm(ks[0], (BATCH, SEQ, D_MODEL), 1.0),
        "x_sample": nrm(ks[1], (DEC_BATCH, DEC_SEQ, D_MODEL), 1.0),
        "g_mix": 1.0 + nrm(ks[2], (DEPTH, D_MODEL), 0.01),
        "w_in": nrm(ks[3], (DEPTH, D_MODEL, IN_WIDTH), D_MODEL ** -0.5),
        "w_out": nrm(ks[4], (DEPTH, MIX_WIDTH, D_MODEL), MIX_WIDTH ** -0.5),
        "g_attn_out": 1.0 + nrm(ks[5], (DEPTH, ATTN_WIDTH), 0.01),
        "g_lru_out": 1.0 + nrm(ks[6], (DEPTH, LRU_WIDTH), 0.01),
        "conv_rg_w": nrm(ks[7], (DEPTH, LRU_CONV_WIDTH, LRU_WIDTH), LRU_CONV_WIDTH ** -0.5),
        "conv_rg_b": nrm(ks[8], (DEPTH, LRU_WIDTH), 0.01),
        "rg_w_a": nrm(ks[9], (DEPTH, 2, N_LRU_BLOCKS, LRU_BLOCK, LRU_BLOCK), LRU_BLOCK ** -0.5),
        "rg_b_a": nrm(ks[10], (DEPTH, 2, LRU_WIDTH), 0.01),
        "rg_w_x": nrm(ks[11], (DEPTH, 2, N_LRU_BLOCKS, LRU_BLOCK, LRU_BLOCK), LRU_BLOCK ** -0.5),
        "rg_b_x": nrm(ks[13], (DEPTH, 2, LRU_WIDTH), 0.01),
        "rg_lam": rg_lam,
        "g_mlp": 1.0 + nrm(ks[14], (DEPTH, D_MODEL), 0.01),
        "w_up": nrm(ks[15], (DEPTH, D_MODEL, 2 * D_FF), D_MODEL ** -0.5),
        "conv_ff_w": nrm(ks[16], (DEPTH, FF_CONV_WIDTH, 2 * D_FF), FF_CONV_WIDTH ** -0.5),
        "conv_ff_b": nrm(ks[17], (DEPTH, 2 * D_FF), 0.01),
        "w_down": nrm(ks[18], (DEPTH, D_FF, D_MODEL), D_FF ** -0.5),
        "g_final": 1.0 + nrm(ks[19], (D_MODEL,), 0.01),
    }


def reference(x_prompt, x_sample, g_mix, w_in, w_out, g_attn_out, g_lru_out, conv_rg_w, conv_rg_b,
              rg_w_a, rg_b_a, rg_w_x, rg_b_x, rg_lam, g_mlp, w_up, conv_ff_w, conv_ff_b, w_down, g_final):
    y_prompt = encode(x_prompt, g_mix, w_in, w_out, g_attn_out, g_lru_out, conv_rg_w, conv_rg_b,
                      rg_w_a, rg_b_a, rg_w_x, rg_b_x, rg_lam, g_mlp, w_up, conv_ff_w, conv_ff_b,
                      w_down, g_final)
    y_sample = encode(x_sample, g_mix, w_in, w_out, g_attn_out, g_lru_out, conv_rg_w, conv_rg_b,
                      rg_w_a, rg_b_a, rg_w_x, rg_b_x, rg_lam, g_mlp, w_up, conv_ff_w, conv_ff_b,
                      w_down, g_final)
    return (y_prompt, y_sample)
```

```python
import functools
import math

import jax
import jax.numpy as jnp
from jax import lax
from jax.experimental import pallas as pl
from jax.experimental.pallas import tpu as pltpu

F32 = jnp.float32
BF16 = jnp.bfloat16

HEAD_DIM = 128
N_HEADS = 8
ATTN_WIDTH = N_HEADS * HEAD_DIM
LRU_WIDTH = 1024
N_LRU_BLOCKS = 8
LRU_BLOCK = LRU_WIDTH // N_LRU_BLOCKS
IN_WIDTH = 3 * ATTN_WIDTH + 2 * LRU_WIDTH
DILATIONS = (1, 4, 16)
KEYS_PER_SIDE = 64
ROPE_THETA = 500000.0
ROPE_HALF = HEAD_DIM // 8
LRU_C = 8.0
NORM_EPS = 1e-6
NEG_INF = -1e30

SUBLANES = 8
BF16_ROWS = 16
VMEM_LIMIT = 56 * 1024 * 1024


def _params(semantics):
    return pltpu.CompilerParams(dimension_semantics=semantics, vmem_limit_bytes=VMEM_LIMIT)


def _rms(x, g):
    return x * lax.rsqrt(jnp.mean(x * x, axis=-1, keepdims=True) + NORM_EPS) * g


def _gelu_tanh(x):
    return x * (0.5 * (1.0 + jnp.tanh(math.sqrt(2.0 / math.pi) * (x + 0.044715 * (x * x * x)))))


def _inproj_kernel(x_ref, g_ref, w_ref, cos_ref, sa_ref, sb_ref, qkv_ref, rg_ref, h_ref):
    j = pl.program_id(1)

    @pl.when(j == 0)
    def _():
        h_ref[...] = _rms(x_ref[...], g_ref[...]).astype(BF16)

    z = jnp.dot(h_ref[...], w_ref[...], preferred_element_type=F32)

    @pl.when(j < 2)
    def _():
        scale = jnp.where(j == 0, HEAD_DIM ** -0.5, 1.0).astype(F32)
        cos = cos_ref[...]
        sa = sa_ref[...]
        sb = sb_ref[...]
        for h in range(N_HEADS):
            sl = slice(h * HEAD_DIM, (h + 1) * HEAD_DIM)
            t = z[:, sl]
            o = t * cos + pltpu.roll(t, HEAD_DIM - ROPE_HALF, 1) * sa + pltpu.roll(t, ROPE_HALF, 1) * sb
            qkv_ref[:, sl] = (o * scale).astype(BF16)

    @pl.when(j == 2)
    def _():
        qkv_ref[...] = z.astype(BF16)

    @pl.when(j >= 3)
    def _():
        rg_ref[...] = z


def _rope_tables(seq):
    inv = ROPE_THETA ** (-jnp.arange(ROPE_HALF, dtype=F32) / ROPE_HALF)
    ang = jnp.arange(seq).astype(F32)[:, None] * inv[None, :]
    cos, sin = jnp.cos(ang), jnp.sin(ang)
    rest = HEAD_DIM - 2 * ROPE_HALF
    cos_t = jnp.concatenate([cos, cos, jnp.ones((seq, rest), F32)], axis=1)
    sa_t = jnp.concatenate([-sin, jnp.zeros((seq, HEAD_DIM - ROPE_HALF), F32)], axis=1)
    sb_t = jnp.concatenate([jnp.zeros((seq, ROPE_HALF), F32), sin, jnp.zeros((seq, rest), F32)], axis=1)
    return cos_t, sa_t, sb_t


def _inproj(x, g_mix, w_in, seq, tm):
    rows, d = x.shape
    tm = min(tm, seq)
    tiles_per_seq = seq // tm
    cos_t, sa_t, sb_t = _rope_tables(seq)
    tab = pl.BlockSpec((tm, HEAD_DIM), lambda i, j: (i % tiles_per_seq, 0))
    return pl.pallas_call(
        _inproj_kernel,
        grid=(rows // tm, IN_WIDTH // ATTN_WIDTH),
        in_specs=[
            pl.BlockSpec((tm, d), lambda i, j: (i, 0)),
            pl.BlockSpec((1, d), lambda i, j: (0, 0)),
            pl.BlockSpec((d, ATTN_WIDTH), lambda i, j: (0, j)),
            tab, tab, tab,
        ],
        out_specs=[
            pl.BlockSpec((tm, ATTN_WIDTH), lambda i, j: (i, jnp.minimum(j, 2))),
            pl.BlockSpec((tm, LRU_WIDTH), lambda i, j: (i, jnp.maximum(j - 3, 0))),
        ],
        out_shape=[
            jax.ShapeDtypeStruct((rows, 3 * ATTN_WIDTH), BF16),
            jax.ShapeDtypeStruct((rows, 2 * LRU_WIDTH), F32),
        ],
        scratch_shapes=[pltpu.VMEM((tm, d), BF16)],
        compiler_params=_params(("parallel", "arbitrary")),
        name="inproj",
    )(x, g_mix, w_in, cos_t, sa_t, sb_t)


def _attn_kernel(q_ref, kp_ref, km_ref, kn_ref, vp_ref, vm_ref, vn_ref, num_ref, ml_ref, *,
                 tq, tiles_per_seq, class_len):
    t = pl.program_id(0) % tiles_per_seq
    nkeys = tq + 2 * KEYS_PER_SIDE
    row = lax.broadcasted_iota(jnp.int32, (tq, nkeys), 0)
    col = lax.broadcasted_iota(jnp.int32, (tq, nkeys), 1)
    rel = col - KEYS_PER_SIDE - row
    key = t * tq - KEYS_PER_SIDE + col
    mask = (jnp.abs(rel) <= KEYS_PER_SIDE) & (key >= 0) & (key < class_len)
    lane = lax.broadcasted_iota(jnp.int32, (tq, HEAD_DIM), 1)
    ml = jnp.zeros((tq, HEAD_DIM), F32)
    for h in range(N_HEADS):
        sl = slice(h * HEAD_DIM, (h + 1) * HEAD_DIM)
        q = q_ref[:, sl]
        k = jnp.concatenate([kp_ref[:, sl], km_ref[:, sl], kn_ref[:, sl]], axis=0)
        v = jnp.concatenate([vp_ref[:, sl], vm_ref[:, sl], vn_ref[:, sl]], axis=0)
        s = lax.dot_general(q, k, (((1,), (1,)), ((), ())), preferred_element_type=F32)
        s = jnp.where(mask, s, NEG_INF)
        m = jnp.max(s, axis=-1, keepdims=True)
        p = jnp.exp(s - m)
        l = jnp.sum(p, axis=-1, keepdims=True)
        num_ref[:, sl] = jnp.dot(p.astype(BF16), v, preferred_element_type=F32)
        ml = jnp.where(lane == h, m, ml)
        ml = jnp.where(lane == N_HEADS + h, l, ml)
    ml_ref[...] = ml


def _attn_branch(qkv, seq, dil, tq):
    rows = qkv.shape[0]
    class_len = seq // dil
    tq = min(tq, class_len)
    tiles_per_seq = class_len // tq
    ratio = tq // KEYS_PER_SIDE
    view = qkv.reshape(rows // dil, dil * 3 * ATTN_WIDTH)
    n_side_blocks = rows // dil // KEYS_PER_SIDE

    def main(c):
        return pl.BlockSpec((tq, ATTN_WIDTH), lambda i, r: (i, 3 * r + c))

    def prev(c):
        return pl.BlockSpec((KEYS_PER_SIDE, ATTN_WIDTH),
                            lambda i, r: (jnp.maximum(i * ratio - 1, 0), 3 * r + c))

    def nxt(c):
        return pl.BlockSpec((KEYS_PER_SIDE, ATTN_WIDTH),
                            lambda i, r: (jnp.minimum((i + 1) * ratio, n_side_blocks - 1), 3 * r + c))

    num, ml = pl.pallas_call(
        functools.partial(_attn_kernel, tq=tq, tiles_per_seq=tiles_per_seq, class_len=class_len),
        grid=(rows // dil // tq, dil),
        in_specs=[main(0), prev(1), main(1), nxt(1), prev(2), main(2), nxt(2)],
        out_specs=[
            pl.BlockSpec((tq, ATTN_WIDTH), lambda i, r: (i, r)),
            pl.BlockSpec((tq, HEAD_DIM), lambda i, r: (i, r)),
        ],
        out_shape=[
            jax.ShapeDtypeStruct((rows // dil, dil * ATTN_WIDTH), F32),
            jax.ShapeDtypeStruct((rows // dil, dil * HEAD_DIM), F32),
        ],
        compiler_params=_params(("parallel", "parallel")),
        name=f"attn_dil{dil}",
    )(view, view, view, view, view, view, view)
    return num.reshape(rows, ATTN_WIDTH), ml.reshape(rows, HEAD_DIM)


def _lru_kernel(*refs, tt, n_tiles, tiles_per_seq, reverse, final):
    if final:
        (xp_ref, xm_ref, xn_ref, gate_ref, other_ref, cw_ref, cb_ref, wax_ref, bax_ref, lamc_ref,
         out_ref, a_ref, b_ref, carry_ref, h_ref) = refs
    else:
        (xp_ref, xm_ref, xn_ref, cw_ref, cb_ref, wax_ref, bax_ref, lamc_ref,
         out_ref, a_ref, b_ref, carry_ref) = refs
        h_ref = out_ref
    i = pl.program_id(0)
    t = ((n_tiles - 1 - i) if reverse else i) % tiles_per_seq

    xp = jnp.where(t == 0, 0.0, xp_ref[...])
    xn = jnp.where(t == tiles_per_seq - 1, 0.0, xn_ref[...])
    xc = jnp.concatenate([xp, xm_ref[...], xn], axis=0)
    u = cb_ref[...]
    for j in range(4):
        u = u + xc[SUBLANES - 2 + j:SUBLANES - 2 + j + tt, :] * cw_ref[j:j + 1, :]
    ub = u.astype(BF16)

    for g in range(N_LRU_BLOCKS):
        sl = slice(g * LRU_BLOCK, (g + 1) * LRU_BLOCK)
        pre = jnp.dot(ub[:, sl], wax_ref[g], preferred_element_type=F32) + bax_ref[g]
        r = jax.nn.sigmoid(pre[:, :LRU_BLOCK])
        ig = jax.nn.sigmoid(pre[:, LRU_BLOCK:])
        log_a = lamc_ref[:, sl] * r
        a = jnp.exp(log_a)
        a_ref[:, sl] = a
        b_ref[:, sl] = jnp.sqrt(jnp.tanh(-log_a) * (1.0 + a * a)) * (ig * u[:, sl])

    n_groups = tt // SUBLANES
    width = a_ref.shape[1]
    rows = lax.broadcasted_iota(jnp.int32, (SUBLANES, width), 0)
    starts_seq = (t == tiles_per_seq - 1) if reverse else (t == 0)
    carry0 = jnp.where(starts_seq, 0.0, carry_ref[...])

    def body(gi, carry):
        g = (n_groups - 1 - gi) if reverse else gi
        off = pl.multiple_of(g * SUBLANES, SUBLANES)
        a = a_ref[pl.ds(off, SUBLANES), :]
        b = b_ref[pl.ds(off, SUBLANES), :]
        for d in (1, 2, 4):
            shift = (SUBLANES - d) if reverse else d
            keep = (rows < SUBLANES - d) if reverse else (rows >= d)
            a_s = pltpu.roll(a, shift, 0)
            b_s = pltpu.roll(b, shift, 0)
            b = jnp.where(keep, a * b_s + b, b)
            a = jnp.where(keep, a * a_s, a)
        h = a * carry + b
        h_ref[pl.ds(off, SUBLANES), :] = h
        edge = h[0:1, :] if reverse else h[SUBLANES - 1:SUBLANES, :]
        return jnp.broadcast_to(edge, (SUBLANES, width))

    carry_ref[...] = lax.fori_loop(0, n_groups, body, carry0)

    if final:
        out_ref[...] = (h_ref[...] + other_ref[...]) * _gelu_tanh(gate_ref[...])


def _lru_pass(rg, other, conv_w, conv_b, wax, bax, lamc, seq, tt, reverse):
    rows = rg.shape[0]
    tt = min(tt, seq)
    n_tiles = rows // tt
    tiles_per_seq = seq // tt
    ratio = tt // SUBLANES
    n_halo_blocks = rows // SUBLANES
    final = other is not None

    def tile(i):
        return (n_tiles - 1 - i) if reverse else i

    x_specs = [
        pl.BlockSpec((SUBLANES, LRU_WIDTH), lambda i: (jnp.maximum(tile(i) * ratio - 1, 0), 0)),
        pl.BlockSpec((tt, LRU_WIDTH), lambda i: (tile(i), 0)),
        pl.BlockSpec((SUBLANES, LRU_WIDTH),
                     lambda i: (jnp.minimum((tile(i) + 1) * ratio, n_halo_blocks - 1), 0)),
    ]
    w_specs = [
        pl.BlockSpec((4, LRU_WIDTH), lambda i: (0, 0)),
        pl.BlockSpec((1, LRU_WIDTH), lambda i: (0, 0)),
        pl.BlockSpec((N_LRU_BLOCKS, LRU_BLOCK, 2 * LRU_BLOCK), lambda i: (0, 0, 0)),
        pl.BlockSpec((N_LRU_BLOCKS, 1, 2 * LRU_BLOCK), lambda i: (0, 0, 0)),
        pl.BlockSpec((1, LRU_WIDTH), lambda i: (0, 0)),
    ]
    scratch = [pltpu.VMEM((tt, LRU_WIDTH), F32), pltpu.VMEM((tt, LRU_WIDTH), F32),
               pltpu.VMEM((SUBLANES, LRU_WIDTH), F32)]
    args = [rg, rg, rg]
    if final:
        x_specs += [pl.BlockSpec((tt, LRU_WIDTH), lambda i: (tile(i), 1)),
                    pl.BlockSpec((tt, LRU_WIDTH), lambda i: (tile(i), 0))]
        args += [rg, other]
        scratch.append(pltpu.VMEM((tt, LRU_WIDTH), F32))
    return pl.pallas_call(
        functools.partial(_lru_kernel, tt=tt, n_tiles=n_tiles, tiles_per_seq=tiles_per_seq,
                          reverse=reverse, final=final),
        grid=(n_tiles,),
        in_specs=x_specs + w_specs,
        out_specs=pl.BlockSpec((tt, LRU_WIDTH), lambda i: (tile(i), 0)),
        out_shape=jax.ShapeDtypeStruct((rows, LRU_WIDTH), F32),
        scratch_shapes=scratch,
        compiler_params=_params(("arbitrary",)),
        name="lru_bwd" if reverse else "lru_fwd",
    )(*args, conv_w, conv_b, wax, bax, lamc)


def _outproj_kernel(x_ref, n1_ref, n2_ref, n3_ref, ml1_ref, ml2_ref, ml3_ref, lru_ref,
                    ga_ref, gl_ref, w_ref, out_ref, attn_ref):
    mls = (ml1_ref[...], ml2_ref[...], ml3_ref[...])
    nums = (n1_ref, n2_ref, n3_ref)
    m_all = jnp.maximum(jnp.maximum(mls[0], mls[1]), mls[2])
    es = [jnp.exp(ml - m_all) for ml in mls]
    den = sum(e * pltpu.roll(ml, HEAD_DIM - N_HEADS, 1) for e, ml in zip(es, mls))
    for h in range(N_HEADS):
        sl = slice(h * HEAD_DIM, (h + 1) * HEAD_DIM)
        acc = sum(e[:, h:h + 1] * n[:, sl] for e, n in zip(es, nums))
        attn_ref[:, sl] = acc / den[:, h:h + 1]
    mixed = jnp.concatenate([_rms(attn_ref[...], ga_ref[...]).astype(BF16),
                             _rms(lru_ref[...], gl_ref[...]).astype(BF16)], axis=1)
    out_ref[...] = x_ref[...] + jnp.dot(mixed, w_ref[...], preferred_element_type=F32)


def _outproj(x, nums, mls, lru, g_attn, g_lru, w_out, tm):
    rows, d = x.shape
    row_block = lambda width: pl.BlockSpec((tm, width), lambda i: (i, 0))
    const = lambda shape: pl.BlockSpec(shape, lambda i: (0, 0))
    return pl.pallas_call(
        _outproj_kernel,
        grid=(rows // tm,),
        in_specs=[row_block(d)] + [row_block(ATTN_WIDTH)] * 3 + [row_block(HEAD_DIM)] * 3
                 + [row_block(LRU_WIDTH), const((1, ATTN_WIDTH)), const((1, LRU_WIDTH)),
                    const(w_out.shape)],
        out_specs=row_block(d),
        out_shape=jax.ShapeDtypeStruct((rows, d), F32),
        scratch_shapes=[pltpu.VMEM((tm, ATTN_WIDTH), F32)],
        compiler_params=_params(("parallel",)),
        name="outproj",
    )(x, *nums, *mls, lru, g_attn, g_lru, w_out)


def _mlp_kernel(xp_ref, xm_ref, xn_ref, g_ref, wg_ref, wv_ref, cwg_ref, cwv_ref, cbg_ref, cbv_ref,
                wd_ref, gf_ref, out_ref, h_ref, *, tm, tiles_per_seq):
    i = pl.program_id(0)
    k = pl.program_id(1)
    t = i % tiles_per_seq
    n = tm + BF16_ROWS

    @pl.when(k == 0)
    def _():
        g = g_ref[...]
        h_ref[0:tm, :] = _rms(xm_ref[...], g).astype(BF16)
        after = jnp.where(t == tiles_per_seq - 1, 0.0, _rms(xn_ref[...], g))
        before = jnp.where(t == 0, 0.0, _rms(xp_ref[...], g))
        h_ref[tm:n, :] = jnp.concatenate([after, before], axis=0).astype(BF16)

    h = h_ref[...]

    def conv(w_ref, cw_ref, cb_ref):
        u = jnp.dot(h, w_ref[...], preferred_element_type=F32)
        c = (cb_ref[...] + pltpu.roll(u, 1, 0) * cw_ref[0:1, :] + u * cw_ref[1:2, :]
             + pltpu.roll(u, n - 1, 0) * cw_ref[2:3, :])
        return c[0:tm, :]

    act = (_gelu_tanh(conv(wg_ref, cwg_ref, cbg_ref)) * conv(wv_ref, cwv_ref, cbv_ref)).astype(BF16)
    part = jnp.dot(act, wd_ref[...], preferred_element_type=F32)

    @pl.when(k == 0)
    def _():
        out_ref[...] = part

    @pl.when(k > 0)
    def _():
        out_ref[...] += part

    @pl.when(k == pl.num_programs(1) - 1)
    def _():
        out_ref[...] = _rms(xm_ref[...] + out_ref[...], gf_ref[...])


def _mlp(x, g_mlp, w_up, conv_w, conv_b, w_down, g_final, seq, tm, tf):
    rows, d = x.shape
    d_ff = w_down.shape[0]
    tm = min(tm, seq)
    tf = min(tf, d_ff)
    nk = d_ff // tf
    tiles_per_seq = seq // tm
    ratio = tm // SUBLANES
    n_halo_blocks = rows // SUBLANES
    return pl.pallas_call(
        functools.partial(_mlp_kernel, tm=tm, tiles_per_seq=tiles_per_seq),
        grid=(rows // tm, nk),
        in_specs=[
            pl.BlockSpec((SUBLANES, d), lambda i, k: (jnp.maximum(i * ratio - 1, 0), 0)),
            pl.BlockSpec((tm, d), lambda i, k: (i, 0)),
            pl.BlockSpec((SUBLANES, d), lambda i, k: (jnp.minimum((i + 1) * ratio, n_halo_blocks - 1), 0)),
            pl.BlockSpec((1, d), lambda i, k: (0, 0)),
            pl.BlockSpec((d, tf), lambda i, k: (0, k)),
            pl.BlockSpec((d, tf), lambda i, k: (0, nk + k)),
            pl.BlockSpec((3, tf), lambda i, k: (0, k)),
            pl.BlockSpec((3, tf), lambda i, k: (0, nk + k)),
            pl.BlockSpec((1, tf), lambda i, k: (0, k)),
            pl.BlockSpec((1, tf), lambda i, k: (0, nk + k)),
            pl.BlockSpec((tf, d), lambda i, k: (k, 0)),
            pl.BlockSpec((1, d), lambda i, k: (0, 0)),
        ],
        out_specs=pl.BlockSpec((tm, d), lambda i, k: (i, 0)),
        out_shape=jax.ShapeDtypeStruct((rows, d), F32),
        scratch_shapes=[pltpu.VMEM((tm + BF16_ROWS, d), BF16)],
        compiler_params=_params(("parallel", "arbitrary")),
        name="mlp",
    )(x, x, x, g_mlp, w_up, w_up, conv_w, conv_w, conv_b, conv_b, w_down, g_final)


def _prepare(g_mix, w_in, w_out, g_attn_out, g_lru_out, conv_rg_w, conv_rg_b, rg_w_a, rg_b_a,
             rg_w_x, rg_b_x, rg_lam, g_mlp, w_up, conv_ff_w, conv_ff_b, w_down, g_final):
    l = 0
    row = lambda v: v.reshape(1, -1).astype(F32)
    wax = jnp.concatenate([rg_w_a[l], rg_w_x[l]], axis=-1).astype(BF16)
    bax = jnp.concatenate([rg_b_a[l].reshape(2, N_LRU_BLOCKS, 1, LRU_BLOCK),
                           rg_b_x[l].reshape(2, N_LRU_BLOCKS, 1, LRU_BLOCK)], axis=-1).astype(F32)
    lamc = (-LRU_C * jax.nn.softplus(-rg_lam[l].astype(F32))).reshape(2, 1, LRU_WIDTH)
    return dict(
        g_mix=row(g_mix[l]), w_in=w_in[l].astype(BF16), w_out=w_out[l].astype(BF16),
        g_attn=row(g_attn_out[l]), g_lru=row(g_lru_out[l]),
        conv_rg_w=conv_rg_w[l].astype(F32), conv_rg_b=row(conv_rg_b[l]),
        wax=wax, bax=bax, lamc=lamc,
        g_mlp=row(g_mlp[l]), w_up=w_up[l].astype(BF16), conv_ff_w=conv_ff_w[l].astype(F32),
        conv_ff_b=row(conv_ff_b[l]), w_down=w_down[l].astype(BF16), g_final=row(g_final),
    )


def _encode(x, p, *, tm_in=512, tq=128, tt=512, tm_out=256, tm_mlp=512, tf=512):
    batch, seq, d = x.shape
    x2 = x.reshape(batch * seq, d)
    qkv, rg = _inproj(x2, p["g_mix"], p["w_in"], seq, tm_in)
    branches = [_attn_branch(qkv, seq, dil, tq) for dil in DILATIONS]
    h_rev = _lru_pass(rg, None, p["conv_rg_w"], p["conv_rg_b"], p["wax"][1], p["bax"][1],
                      p["lamc"][1], seq, tt, reverse=True)
    lru = _lru_pass(rg, h_rev, p["conv_rg_w"], p["conv_rg_b"], p["wax"][0], p["bax"][0],
                    p["lamc"][0], seq, tt, reverse=False)
    x1 = _outproj(x2, [b[0] for b in branches], [b[1] for b in branches], lru,
                  p["g_attn"], p["g_lru"], p["w_out"], min(tm_out, seq))
    y = _mlp(x1, p["g_mlp"], p["w_up"], p["conv_ff_w"], p["conv_ff_b"], p["w_down"], p["g_final"],
             seq, tm_mlp, tf)
    return y.reshape(batch, seq, d)


def kernel(x_prompt, x_sample, g_mix, w_in, w_out, g_attn_out, g_lru_out, conv_rg_w, conv_rg_b,
           rg_w_a, rg_b_a, rg_w_x, rg_b_x, rg_lam, g_mlp, w_up, conv_ff_w, conv_ff_b, w_down, g_final):
    p = _prepare(g_mix, w_in, w_out, g_attn_out, g_lru_out, conv_rg_w, conv_rg_b, rg_w_a, rg_b_a,
                 rg_w_x, rg_b_x, rg_lam, g_mlp, w_up, conv_ff_w, conv_ff_b, w_down, g_final)
    return (_encode(x_prompt, p), _encode(x_sample, p))
```

```python
import functools
import math

import jax
import jax.numpy as jnp
from jax import lax
from jax.experimental import pallas as pl
from jax.experimental.pallas import tpu as pltpu

F32 = jnp.float32
BF16 = jnp.bfloat16

HEAD_DIM = 128
N_HEADS = 8
ATTN_WIDTH = N_HEADS * HEAD_DIM
LRU_WIDTH = 1024
N_LRU_BLOCKS = 8
LRU_BLOCK = LRU_WIDTH // N_LRU_BLOCKS
IN_WIDTH = 3 * ATTN_WIDTH + 2 * LRU_WIDTH
MAX_DIL = 16
KEYS_PER_SIDE = 64
UNIT = MAX_DIL * KEYS_PER_SIDE
CLASS_ROWS = UNIT // MAX_DIL
TQ1 = 128
ATTN_SKEW = 2
IN_TILE = ATTN_WIDTH
IN_CHUNK = 256
ROPE_THETA = 500000.0
ROPE_HALF = HEAD_DIM // 8
LRU_C = 8.0
NORM_EPS = 1e-6
NEG_INF = -1e30

SUBLANES = 8
BF16_ROWS = 16
VMEM_LIMIT = 56 * 1024 * 1024


def _params(semantics):
    return pltpu.CompilerParams(dimension_semantics=semantics, vmem_limit_bytes=VMEM_LIMIT)


def _rms(x, g):
    return x * lax.rsqrt(jnp.mean(x * x, axis=-1, keepdims=True) + NORM_EPS) * g


def _gelu_tanh(x):
    return x * (0.5 * (1.0 + jnp.tanh(math.sqrt(2.0 / math.pi) * (x + 0.044715 * (x * x * x)))))


def _inproj_kernel(x_ref, g_ref, w_ref, cos_ref, sa_ref, sb_ref, nat_ref, cm_ref, rg_ref, h_ref, z_ref):
    j = pl.program_id(1)

    @pl.when(j == 0)
    def _():
        h_ref[...] = _rms(x_ref[...], g_ref[...]).astype(BF16)

    def project(c):
        cols = slice(c * IN_CHUNK, (c + 1) * IN_CHUNK)
        return jnp.dot(h_ref[...], w_ref[:, cols], preferred_element_type=F32)

    n_chunks = IN_TILE // IN_CHUNK

    @pl.when(j < 3)
    def _():
        scale = jnp.where(j == 0, HEAD_DIM ** -0.5, 1.0).astype(F32)
        is_v = j == 2
        cos = jnp.where(is_v, 1.0, cos_ref[...]) * scale
        sa = jnp.where(is_v, 0.0, sa_ref[...]) * scale
        sb = jnp.where(is_v, 0.0, sb_ref[...]) * scale
        z_next = project(0)
        for c in range(n_chunks):
            z = z_next
            if c + 1 < n_chunks:
                z_next = project(c + 1)
            for hc in range(IN_CHUNK // HEAD_DIM):
                hh = c * (IN_CHUNK // HEAD_DIM) + hc
                t = z[:, hc * HEAD_DIM:(hc + 1) * HEAD_DIM]
                t = t * cos + pltpu.roll(t, HEAD_DIM - ROPE_HALF, 1) * sa + pltpu.roll(t, ROPE_HALF, 1) * sb
                nat_ref[hh] = t.astype(BF16)
                z_ref[hh * UNIT:(hh + 1) * UNIT, :] = t
                for r in range(MAX_DIL):
                    piece = z_ref[pl.ds(hh * UNIT + r, CLASS_ROWS, stride=MAX_DIL), :]
                    cm_ref[hh, r * CLASS_ROWS:(r + 1) * CLASS_ROWS, :] = piece.astype(BF16)

    @pl.when(j >= 3)
    def _():
        for c in range(n_chunks):
            rg_ref[:, c * IN_CHUNK:(c + 1) * IN_CHUNK] = project(c)


def _rope_tables(seq):
    inv = ROPE_THETA ** (-jnp.arange(ROPE_HALF, dtype=F32) / ROPE_HALF)
    ang = jnp.arange(seq).astype(F32)[:, None] * inv[None, :]
    cos, sin = jnp.cos(ang), jnp.sin(ang)
    rest = HEAD_DIM - 2 * ROPE_HALF
    cos_t = jnp.concatenate([cos, cos, jnp.ones((seq, rest), F32)], axis=1)
    sa_t = jnp.concatenate([-sin, jnp.zeros((seq, HEAD_DIM - ROPE_HALF), F32)], axis=1)
    sb_t = jnp.concatenate([jnp.zeros((seq, ROPE_HALF), F32), sin, jnp.zeros((seq, rest), F32)], axis=1)
    return cos_t, sa_t, sb_t


def _inproj(x, g_mix, w_in, seq):
    rows, d = x.shape
    units_per_seq = seq // UNIT
    cos_t, sa_t, sb_t = _rope_tables(seq)
    tab = pl.BlockSpec((UNIT, HEAD_DIM), lambda i, j: (i % units_per_seq, 0))
    heads = pl.BlockSpec((N_HEADS, UNIT, HEAD_DIM), lambda i, j: (jnp.minimum(j, 2), i, 0))
    planes = jax.ShapeDtypeStruct((3 * N_HEADS, rows, HEAD_DIM), BF16)
    return pl.pallas_call(
        _inproj_kernel,
        grid=(rows // UNIT, IN_WIDTH // IN_TILE),
        in_specs=[
            pl.BlockSpec((UNIT, d), lambda i, j: (i, 0)),
            pl.BlockSpec((1, d), lambda i, j: (0, 0)),
            pl.BlockSpec((d, IN_TILE), lambda i, j: (0, j)),
            tab, tab, tab,
        ],
        out_specs=[
            heads, heads,
            pl.BlockSpec((UNIT, IN_TILE), lambda i, j: (i, jnp.maximum(j - 3, 0))),
        ],
        out_shape=[planes, planes, jax.ShapeDtypeStruct((rows, 2 * LRU_WIDTH), F32)],
        scratch_shapes=[pltpu.VMEM((UNIT, d), BF16), pltpu.VMEM((N_HEADS * UNIT, HEAD_DIM), F32)],
        compiler_params=_params(("parallel", "arbitrary")),
        name="inproj",
    )(x, g_mix, w_in, cos_t, sa_t, sb_t)


def _attn_kernel(qn_ref, knp_ref, knm_ref, knn_ref, vnp_ref, vnm_ref, vnn_ref,
                 qc_ref, kcp_ref, kcm_ref, kcn_ref, vcp_ref, vcm_ref, vcn_ref,
                 out_ref, n1_ref, m1_ref, l1_ref, num_ref, max_ref, den_ref, *, units_per_seq):
    u = pl.program_id(1) % units_per_seq
    has_prev = u > 0
    has_next = u < units_per_seq - 1
    side = KEYS_PER_SIDE
    cat = lambda pieces: jnp.concatenate(pieces, axis=0)
    tiles = []

    n_tiles = UNIT // TQ1
    nk1 = TQ1 + 2 * side
    row = lax.broadcasted_iota(jnp.int32, (TQ1, nk1), 0)
    col = lax.broadcasted_iota(jnp.int32, (TQ1, nk1), 1)
    band = jnp.abs(col - side - row) <= side

    def tile1(c):
        lo, hi = c * TQ1 - side, c * TQ1 + TQ1 + side
        rows_c = slice(c * TQ1, (c + 1) * TQ1)
        if c == 0:
            kv = lambda p, m, n: cat([p[...], m[0:hi, :]])
            mask = band & ((col >= side) | has_prev)
        elif c == n_tiles - 1:
            kv = lambda p, m, n: cat([m[lo:UNIT, :], n[...]])
            mask = band & ((col < TQ1 + side) | has_next)
        else:
            kv = lambda p, m, n: m[lo:hi, :]
            mask = band

        def finish(num, m, l):
            n1_ref[rows_c, :] = num
            m1_ref[rows_c, :] = jnp.broadcast_to(m, (TQ1, HEAD_DIM))
            l1_ref[rows_c, :] = jnp.broadcast_to(l, (TQ1, HEAD_DIM))

        return (lambda: qn_ref[rows_c, :], lambda: kv(knp_ref, knm_ref, knn_ref),
                lambda: kv(vnp_ref, vnm_ref, vnn_ref), mask, finish)

    tiles += [tile1(c) for c in range(n_tiles)]

    nk16 = 3 * CLASS_ROWS
    row = lax.broadcasted_iota(jnp.int32, (CLASS_ROWS, nk16), 0)
    col = lax.broadcasted_iota(jnp.int32, (CLASS_ROWS, nk16), 1)
    mask16 = ((jnp.abs(col - CLASS_ROWS - row) <= side) & ((col >= CLASS_ROWS) | has_prev)
              & ((col < 2 * CLASS_ROWS) | has_next))

    def tile16(r):
        sl = slice(r * CLASS_ROWS, (r + 1) * CLASS_ROWS)
        strided = pl.ds(r, CLASS_ROWS, stride=MAX_DIL)

        def finish(num, m, l):
            m1 = m1_ref[strided, :]
            m_new = jnp.maximum(m1, m)
            e1 = jnp.exp(m1 - m_new)
            e = jnp.exp(m - m_new)
            num_ref[sl, :] = e1 * n1_ref[strided, :] + e * num
            den_ref[sl, :] = e1 * l1_ref[strided, :] + e * l
            max_ref[sl, :] = m_new

        return (lambda: qc_ref[sl, :], lambda: cat([kcp_ref[sl, :], kcm_ref[sl, :], kcn_ref[sl, :]]),
                lambda: cat([vcp_ref[sl, :], vcm_ref[sl, :], vcn_ref[sl, :]]), mask16, finish)

    tiles += [tile16(r) for r in range(MAX_DIL)]

    edge = side // 4
    nq4 = 4 * CLASS_ROWS
    nk4 = nq4 + 2 * side
    row = lax.broadcasted_iota(jnp.int32, (nq4, nk4), 0)
    col = lax.broadcasted_iota(jnp.int32, (nq4, nk4), 1)
    q_idx = 4 * (row & (CLASS_ROWS - 1)) + (row >> 6)
    c_mid = col - side
    c_next = col - side - nq4
    k_idx = jnp.where(
        col < side, 4 * (CLASS_ROWS - edge + (col & (edge - 1))) + (col >> 4) - nq4,
        jnp.where(col < side + nq4, 4 * (c_mid & (CLASS_ROWS - 1)) + (c_mid >> 6),
                  4 * (c_next & (edge - 1)) + (c_next >> 4) + nq4))
    mask4 = ((jnp.abs(k_idx - q_idx) <= side) & ((col >= side) | has_prev)
             & ((col < side + nq4) | has_next))
    def tile4(r4):
        slabs = [slice((4 * s + r4) * CLASS_ROWS, (4 * s + r4 + 1) * CLASS_ROWS) for s in range(4)]

        def gather(ref, part):
            lo, hi = {"tail": (CLASS_ROWS - edge, CLASS_ROWS), "all": (0, CLASS_ROWS), "head": (0, edge)}[part]
            return [ref[sl.start + lo:sl.start + hi, :] for sl in slabs]

        def kv(p, m, n):
            return cat(gather(p, "tail") + gather(m, "all") + gather(n, "head"))

        def finish(num, m, l):
            m_old = cat([max_ref[sl, :] for sl in slabs])
            m_new = jnp.maximum(m_old, m)
            e_old = jnp.exp(m_old - m_new)
            e = jnp.exp(m - m_new)
            res = ((e_old * cat([num_ref[sl, :] for sl in slabs]) + e * num)
                   / (e_old * cat([den_ref[sl, :] for sl in slabs]) + e * l))
            for s in range(4):
                out_ref[pl.ds(4 * s + r4, CLASS_ROWS, stride=MAX_DIL), :] = res[CLASS_ROWS * s:CLASS_ROWS * (s + 1), :]

        return (lambda: cat(gather(qc_ref, "all")), lambda: kv(kcp_ref, kcm_ref, kcn_ref),
                lambda: kv(vcp_ref, vcm_ref, vcn_ref), mask4, finish)

    tiles += [tile4(r4) for r4 in range(4)]

    def scores(tile):
        return lax.dot_general(tile[0](), tile[1](), (((1,), (1,)), ((), ())), preferred_element_type=F32)

    pending = [scores(t) for t in tiles[:ATTN_SKEW]]
    for idx, (_, _, v, mask, finish) in enumerate(tiles):
        s = pending.pop(0)
        if idx + ATTN_SKEW < len(tiles):
            pending.append(scores(tiles[idx + ATTN_SKEW]))
        s = jnp.where(mask, s, NEG_INF)
        m = jnp.max(s, axis=-1, keepdims=True)
        p = jnp.exp(s - m)
        l = jnp.sum(p, axis=-1, keepdims=True)
        finish(jnp.dot(p.astype(BF16), v(), preferred_element_type=F32), m, l)


def _attention(nat, cm, seq):
    rows = nat.shape[1]
    units_per_seq = seq // UNIT
    ratio = UNIT // KEYS_PER_SIDE
    n_side_blocks = rows // KEYS_PER_SIDE
    n_units = rows // UNIT

    def unit(plane, shift):
        return pl.BlockSpec((None, UNIT, HEAD_DIM),
                            lambda h, u: (plane * N_HEADS + h, jnp.clip(u + shift, 0, n_units - 1), 0))

    def halo(plane, after):
        if after:
            index = lambda h, u: (plane * N_HEADS + h, jnp.minimum((u + 1) * ratio, n_side_blocks - 1), 0)
        else:
            index = lambda h, u: (plane * N_HEADS + h, jnp.maximum(u * ratio - 1, 0), 0)
        return pl.BlockSpec((None, KEYS_PER_SIDE, HEAD_DIM), index)

    nat_specs = [unit(0, 0), halo(1, False), unit(1, 0), halo(1, True), halo(2, False), unit(2, 0), halo(2, True)]
    cm_specs = [unit(0, 0), unit(1, -1), unit(1, 0), unit(1, 1), unit(2, -1), unit(2, 0), unit(2, 1)]
    return pl.pallas_call(
        functools.partial(_attn_kernel, units_per_seq=units_per_seq),
        grid=(N_HEADS, n_units),
        in_specs=nat_specs + cm_specs,
        out_specs=pl.BlockSpec((UNIT, HEAD_DIM), lambda h, u: (u, h)),
        out_shape=jax.ShapeDtypeStruct((rows, ATTN_WIDTH), F32),
        scratch_shapes=[pltpu.VMEM((UNIT, HEAD_DIM), F32)] * 6,
        compiler_params=_params(("parallel", "parallel")),
        name="attention",
    )(*([nat] * 7), *([cm] * 7))


def _lru_kernel(*refs, tt, n_tiles, tiles_per_seq, reverse, final):
    if final:
        (xp_ref, xm_ref, xn_ref, gate_ref, other_ref, cw_ref, cb_ref, wax_ref, bax_ref, lamc_ref,
         out_ref, a_ref, b_ref, carry_ref, h_ref) = refs
    else:
        (xp_ref, xm_ref, xn_ref, cw_ref, cb_ref, wax_ref, bax_ref, lamc_ref,
         out_ref, a_ref, b_ref, carry_ref) = refs
        h_ref = out_ref
    i = pl.program_id(0)
    t = ((n_tiles - 1 - i) if reverse else i) % tiles_per_seq

    xp = jnp.where(t == 0, 0.0, xp_ref[...])
    xn = jnp.where(t == tiles_per_seq - 1, 0.0, xn_ref[...])
    xc = jnp.concatenate([xp, xm_ref[...], xn], axis=0)
    u = cb_ref[...]
    for j in range(4):
        u = u + xc[SUBLANES - 2 + j:SUBLANES - 2 + j + tt, :] * cw_ref[j:j + 1, :]
    ub = u.astype(BF16)

    for g in range(N_LRU_BLOCKS):
        sl = slice(g * LRU_BLOCK, (g + 1) * LRU_BLOCK)
        pre = jnp.dot(ub[:, sl], wax_ref[g], preferred_element_type=F32) + bax_ref[g]
        r = jax.nn.sigmoid(pre[:, :LRU_BLOCK])
        ig = jax.nn.sigmoid(pre[:, LRU_BLOCK:])
        log_a = lamc_ref[:, sl] * r
        a = jnp.exp(log_a)
        a_ref[:, sl] = a
        b_ref[:, sl] = jnp.sqrt(jnp.tanh(-log_a) * (1.0 + a * a)) * (ig * u[:, sl])

    n_groups = tt // SUBLANES
    width = a_ref.shape[1]
    rows = lax.broadcasted_iota(jnp.int32, (SUBLANES, width), 0)
    starts_seq = (t == tiles_per_seq - 1) if reverse else (t == 0)
    carry0 = jnp.where(starts_seq, 0.0, carry_ref[...])

    def body(gi, carry):
        g = (n_groups - 1 - gi) if reverse else gi
        off = pl.multiple_of(g * SUBLANES, SUBLANES)
        a = a_ref[pl.ds(off, SUBLANES), :]
        b = b_ref[pl.ds(off, SUBLANES), :]
        for d in (1, 2, 4):
            shift = (SUBLANES - d) if reverse else d
            keep = (rows < SUBLANES - d) if reverse else (rows >= d)
            a_s = pltpu.roll(a, shift, 0)
            b_s = pltpu.roll(b, shift, 0)
            b = jnp.where(keep, a * b_s + b, b)
            a = jnp.where(keep, a * a_s, a)
        h = a * carry + b
        h_ref[pl.ds(off, SUBLANES), :] = h
        edge = h[0:1, :] if reverse else h[SUBLANES - 1:SUBLANES, :]
        return jnp.broadcast_to(edge, (SUBLANES, width))

    carry_ref[...] = lax.fori_loop(0, n_groups, body, carry0)

    if final:
        out_ref[...] = (h_ref[...] + other_ref[...]) * _gelu_tanh(gate_ref[...])


def _lru_pass(rg, other, conv_w, conv_b, wax, bax, lamc, seq, tt, reverse):
    rows = rg.shape[0]
    tt = min(tt, seq)
    n_tiles = rows // tt
    tiles_per_seq = seq // tt
    ratio = tt // SUBLANES
    n_halo_blocks = rows // SUBLANES
    final = other is not None

    def tile(i):
        return (n_tiles - 1 - i) if reverse else i

    x_specs = [
        pl.BlockSpec((SUBLANES, LRU_WIDTH), lambda i: (jnp.maximum(tile(i) * ratio - 1, 0), 0)),
        pl.BlockSpec((tt, LRU_WIDTH), lambda i: (tile(i), 0)),
        pl.BlockSpec((SUBLANES, LRU_WIDTH),
                     lambda i: (jnp.minimum((tile(i) + 1) * ratio, n_halo_blocks - 1), 0)),
    ]
    w_specs = [
        pl.BlockSpec((4, LRU_WIDTH), lambda i: (0, 0)),
        pl.BlockSpec((1, LRU_WIDTH), lambda i: (0, 0)),
        pl.BlockSpec((N_LRU_BLOCKS, LRU_BLOCK, 2 * LRU_BLOCK), lambda i: (0, 0, 0)),
        pl.BlockSpec((N_LRU_BLOCKS, 1, 2 * LRU_BLOCK), lambda i: (0, 0, 0)),
        pl.BlockSpec((1, LRU_WIDTH), lambda i: (0, 0)),
    ]
    scratch = [pltpu.VMEM((tt, LRU_WIDTH), F32), pltpu.VMEM((tt, LRU_WIDTH), F32),
               pltpu.VMEM((SUBLANES, LRU_WIDTH), F32)]
    args = [rg, rg, rg]
    if final:
        x_specs += [pl.BlockSpec((tt, LRU_WIDTH), lambda i: (tile(i), 1)),
                    pl.BlockSpec((tt, LRU_WIDTH), lambda i: (tile(i), 0))]
        args += [rg, other]
        scratch.append(pltpu.VMEM((tt, LRU_WIDTH), F32))
    return pl.pallas_call(
        functools.partial(_lru_kernel, tt=tt, n_tiles=n_tiles, tiles_per_seq=tiles_per_seq,
                          reverse=reverse, final=final),
        grid=(n_tiles,),
        in_specs=x_specs + w_specs,
        out_specs=pl.BlockSpec((tt, LRU_WIDTH), lambda i: (tile(i), 0)),
        out_shape=jax.ShapeDtypeStruct((rows, LRU_WIDTH), F32),
        scratch_shapes=scratch,
        compiler_params=_params(("arbitrary",)),
        name="lru_bwd" if reverse else "lru_fwd",
    )(*args, conv_w, conv_b, wax, bax, lamc)


def _outproj_kernel(x_ref, attn_ref, lru_ref, ga_ref, gl_ref, w_ref, out_ref):
    mixed = jnp.concatenate([_rms(attn_ref[...], ga_ref[...]).astype(BF16),
                             _rms(lru_ref[...], gl_ref[...]).astype(BF16)], axis=1)
    out_ref[...] = x_ref[...] + jnp.dot(mixed, w_ref[...], preferred_element_type=F32)


def _outproj(x, attn, lru, g_attn, g_lru, w_out, tm):
    rows, d = x.shape
    row_block = lambda width: pl.BlockSpec((tm, width), lambda i: (i, 0))
    const = lambda shape: pl.BlockSpec(shape, lambda i: (0, 0))
    return pl.pallas_call(
        _outproj_kernel,
        grid=(rows // tm,),
        in_specs=[row_block(d), row_block(ATTN_WIDTH), row_block(LRU_WIDTH),
                  const((1, ATTN_WIDTH)), const((1, LRU_WIDTH)), const(w_out.shape)],
        out_specs=row_block(d),
        out_shape=jax.ShapeDtypeStruct((rows, d), F32),
        compiler_params=_params(("parallel",)),
        name="outproj",
    )(x, attn, lru, g_attn, g_lru, w_out)


def _mlp_kernel(xp_ref, xm_ref, xn_ref, g_ref, wg_ref, wv_ref, cwg_ref, cwv_ref, cbg_ref, cbv_ref,
                wd_ref, gf_ref, out_ref, h_ref, *, tm, tiles_per_seq):
    i = pl.program_id(0)
    k = pl.program_id(1)
    t = i % tiles_per_seq
    n = tm + BF16_ROWS

    @pl.when(k == 0)
    def _():
        g = g_ref[...]
        h_ref[0:tm, :] = _rms(xm_ref[...], g).astype(BF16)
        after = jnp.where(t == tiles_per_seq - 1, 0.0, _rms(xn_ref[...], g))
        before = jnp.where(t == 0, 0.0, _rms(xp_ref[...], g))
        h_ref[tm:n, :] = jnp.concatenate([after, before], axis=0).astype(BF16)

    h = h_ref[...]

    def conv(w_ref, cw_ref, cb_ref):
        u = jnp.dot(h, w_ref[...], preferred_element_type=F32)
        c = (cb_ref[...] + pltpu.roll(u, 1, 0) * cw_ref[0:1, :] + u * cw_ref[1:2, :]
             + pltpu.roll(u, n - 1, 0) * cw_ref[2:3, :])
        return c[0:tm, :]

    act = (_gelu_tanh(conv(wg_ref, cwg_ref, cbg_ref)) * conv(wv_ref, cwv_ref, cbv_ref)).astype(BF16)
    part = jnp.dot(act, wd_ref[...], preferred_element_type=F32)

    @pl.when(k == 0)
    def _():
        out_ref[...] = part

    @pl.when(k > 0)
    def _():
        out_ref[...] += part

    @pl.when(k == pl.num_programs(1) - 1)
    def _():
        out_ref[...] = _rms(xm_ref[...] + out_ref[...], gf_ref[...])


def _mlp(x, g_mlp, w_up, conv_w, conv_b, w_down, g_final, seq, tm, tf):
    rows, d = x.shape
    d_ff = w_down.shape[0]
    tm = min(tm, seq)
    tf = min(tf, d_ff)
    nk = d_ff // tf
    tiles_per_seq = seq // tm
    ratio = tm // SUBLANES
    n_halo_blocks = rows // SUBLANES
    return pl.pallas_call(
        functools.partial(_mlp_kernel, tm=tm, tiles_per_seq=tiles_per_seq),
        grid=(rows // tm, nk),
        in_specs=[
            pl.BlockSpec((SUBLANES, d), lambda i, k: (jnp.maximum(i * ratio - 1, 0), 0)),
            pl.BlockSpec((tm, d), lambda i, k: (i, 0)),
            pl.BlockSpec((SUBLANES, d), lambda i, k: (jnp.minimum((i + 1) * ratio, n_halo_blocks - 1), 0)),
            pl.BlockSpec((1, d), lambda i, k: (0, 0)),
            pl.BlockSpec((d, tf), lambda i, k: (0, k)),
            pl.BlockSpec((d, tf), lambda i, k: (0, nk + k)),
            pl.BlockSpec((3, tf), lambda i, k: (0, k)),
            pl.BlockSpec((3, tf), lambda i, k: (0, nk + k)),
            pl.BlockSpec((1, tf), lambda i, k: (0, k)),
            pl.BlockSpec((1, tf), lambda i, k: (0, nk + k)),
            pl.BlockSpec((tf, d), lambda i, k: (k, 0)),
            pl.BlockSpec((1, d), lambda i, k: (0, 0)),
        ],
        out_specs=pl.BlockSpec((tm, d), lambda i, k: (i, 0)),
        out_shape=jax.ShapeDtypeStruct((rows, d), F32),
        scratch_shapes=[pltpu.VMEM((tm + BF16_ROWS, d), BF16)],
        compiler_params=_params(("parallel", "arbitrary")),
        name="mlp",
    )(x, x, x, g_mlp, w_up, w_up, conv_w, conv_w, conv_b, conv_b, w_down, g_final)


def _prepare(g_mix, w_in, w_out, g_attn_out, g_lru_out, conv_rg_w, conv_rg_b, rg_w_a, rg_b_a,
             rg_w_x, rg_b_x, rg_lam, g_mlp, w_up, conv_ff_w, conv_ff_b, w_down, g_final):
    l = 0
    row = lambda v: v.reshape(1, -1).astype(F32)
    wax = jnp.concatenate([rg_w_a[l], rg_w_x[l]], axis=-1).astype(BF16)
    bax = jnp.concatenate([rg_b_a[l].reshape(2, N_LRU_BLOCKS, 1, LRU_BLOCK),
                           rg_b_x[l].reshape(2, N_LRU_BLOCKS, 1, LRU_BLOCK)], axis=-1).astype(F32)
    lamc = (-LRU_C * jax.nn.softplus(-rg_lam[l].astype(F32))).reshape(2, 1, LRU_WIDTH)
    return dict(
        g_mix=row(g_mix[l]), w_in=w_in[l].astype(BF16), w_out=w_out[l].astype(BF16),
        g_attn=row(g_attn_out[l]), g_lru=row(g_lru_out[l]),
        conv_rg_w=conv_rg_w[l].astype(F32), conv_rg_b=row(conv_rg_b[l]),
        wax=wax, bax=bax, lamc=lamc,
        g_mlp=row(g_mlp[l]), w_up=w_up[l].astype(BF16), conv_ff_w=conv_ff_w[l].astype(F32),
        conv_ff_b=row(conv_ff_b[l]), w_down=w_down[l].astype(BF16), g_final=row(g_final),
    )


def _encode(x, p, *, tt=512, tm_out=256, tm_mlp=512, tf=1024):
    batch, seq, d = x.shape
    x2 = x.reshape(batch * seq, d)
    nat, cm, rg = _inproj(x2, p["g_mix"], p["w_in"], seq)
    attn = _attention(nat, cm, seq)
    h_rev = _lru_pass(rg, None, p["conv_rg_w"], p["conv_rg_b"], p["wax"][1], p["bax"][1],
                      p["lamc"][1], seq, tt, reverse=True)
    lru = _lru_pass(rg, h_rev, p["conv_rg_w"], p["conv_rg_b"], p["wax"][0], p["bax"][0],
                    p["lamc"][0], seq, tt, reverse=False)
    x1 = _outproj(x2, attn, lru, p["g_attn"], p["g_lru"], p["w_out"], min(tm_out, seq))
    y = _mlp(x1, p["g_mlp"], p["w_up"], p["conv_ff_w"], p["conv_ff_b"], p["w_down"], p["g_final"],
             seq, tm_mlp, tf)
    return y.reshape(batch, seq, d)


def kernel(x_prompt, x_sample, g_mix, w_in, w_out, g_attn_out, g_lru_out, conv_rg_w, conv_rg_b,
           rg_w_a, rg_b_a, rg_w_x, rg_b_x, rg_lam, g_mlp, w_up, conv_ff_w, conv_ff_b, w_down, g_final):
    p = _prepare(g_mix, w_in, w_out, g_attn_out, g_lru_out, conv_rg_w, conv_rg_b, rg_w_a, rg_b_a,
                 rg_w_x, rg_b_x, rg_lam, g_mlp, w_up, conv_ff_w, conv_ff_b, w_down, g_final)
    return (_encode(x_prompt, p), _encode(x_sample, p))
```

```python
import functools
import math

import jax
import jax.numpy as jnp
from jax import lax
from jax.experimental import pallas as pl
from jax.experimental.pallas import tpu as pltpu

F32 = jnp.float32
BF16 = jnp.bfloat16

HEAD_DIM = 128
N_HEADS = 8
ATTN_WIDTH = N_HEADS * HEAD_DIM
LRU_WIDTH = 1024
N_LRU_BLOCKS = 8
LRU_BLOCK = LRU_WIDTH // N_LRU_BLOCKS
IN_WIDTH = 3 * ATTN_WIDTH + 2 * LRU_WIDTH
MAX_DIL = 16
KEYS_PER_SIDE = 64
UNIT = MAX_DIL * KEYS_PER_SIDE
CLASS_ROWS = UNIT // MAX_DIL
TQ1 = 128
ATTN_SKEW = 2
IN_ROWS = 512
IN_CHUNK = 256
ROPE_THETA = 500000.0
ROPE_HALF = HEAD_DIM // 8
LRU_C = 8.0
NORM_EPS = 1e-6
NEG_INF = -1e30

SUBLANES = 8
BF16_ROWS = 16
VMEM_LIMIT = 56 * 1024 * 1024


def _params(semantics):
    return pltpu.CompilerParams(dimension_semantics=semantics, vmem_limit_bytes=VMEM_LIMIT)


def _rms(x, g):
    return x * lax.rsqrt(jnp.mean(x * x, axis=-1, keepdims=True) + NORM_EPS) * g


def _gelu_tanh(x):
    return x * (0.5 * (1.0 + jnp.tanh(math.sqrt(2.0 / math.pi) * (x + 0.044715 * (x * x * x)))))


def _inproj_kernel(x_ref, g_ref, w_ref, cos_ref, sa_ref, sb_ref, nat_ref, cm_ref, rg_ref, h_ref, z_ref):
    h_ref[...] = _rms(x_ref[...], g_ref[...]).astype(BF16)

    def project(c):
        cols = slice(c * IN_CHUNK, (c + 1) * IN_CHUNK)
        return jnp.dot(h_ref[...], w_ref[:, cols], preferred_element_type=F32)

    heads_per_chunk = IN_CHUNK // HEAD_DIM
    n_chunks = IN_WIDTH // IN_CHUNK
    n_qkv_chunks = 3 * ATTN_WIDTH // IN_CHUNK
    half_rows = IN_ROWS // MAX_DIL
    scale = HEAD_DIM ** -0.5
    rotary = {0: (cos_ref[...] * scale, sa_ref[...] * scale, sb_ref[...] * scale),
              1: (cos_ref[...], sa_ref[...], sb_ref[...])}

    z_next = project(0)
    for c in range(n_chunks):
        z = z_next
        if c + 1 < n_chunks:
            z_next = project(c + 1)
        if c >= n_qkv_chunks:
            off = (c - n_qkv_chunks) * IN_CHUNK
            rg_ref[:, off:off + IN_CHUNK] = z
            continue
        for hc in range(heads_per_chunk):
            plane = c * heads_per_chunk + hc
            t = z[:, hc * HEAD_DIM:(hc + 1) * HEAD_DIM]
            if plane // N_HEADS in rotary:
                cos, sa, sb = rotary[plane // N_HEADS]
                t = t * cos + pltpu.roll(t, HEAD_DIM - ROPE_HALF, 1) * sa + pltpu.roll(t, ROPE_HALF, 1) * sb
            nat_ref[plane] = t.astype(BF16)
            slot = plane % 2
            z_ref[slot * IN_ROWS:(slot + 1) * IN_ROWS, :] = t
            for r in range(MAX_DIL):
                piece = z_ref[pl.ds(slot * IN_ROWS + r, half_rows, stride=MAX_DIL), :]
                cm_ref[plane, r] = piece.astype(BF16)


def _rope_tables(seq):
    inv = ROPE_THETA ** (-jnp.arange(ROPE_HALF, dtype=F32) / ROPE_HALF)
    ang = jnp.arange(seq).astype(F32)[:, None] * inv[None, :]
    cos, sin = jnp.cos(ang), jnp.sin(ang)
    rest = HEAD_DIM - 2 * ROPE_HALF
    cos_t = jnp.concatenate([cos, cos, jnp.ones((seq, rest), F32)], axis=1)
    sa_t = jnp.concatenate([-sin, jnp.zeros((seq, HEAD_DIM - ROPE_HALF), F32)], axis=1)
    sb_t = jnp.concatenate([jnp.zeros((seq, ROPE_HALF), F32), sin, jnp.zeros((seq, rest), F32)], axis=1)
    return cos_t, sa_t, sb_t


def _inproj(x, g_mix, w_in, seq):
    rows, d = x.shape
    tiles_per_seq = seq // IN_ROWS
    tiles_per_unit = UNIT // IN_ROWS
    half_rows = IN_ROWS // MAX_DIL
    cos_t, sa_t, sb_t = _rope_tables(seq)
    tab = pl.BlockSpec((IN_ROWS, HEAD_DIM), lambda i: (i % tiles_per_seq, 0))
    n_planes = 3 * N_HEADS
    nat, cm, rg = pl.pallas_call(
        _inproj_kernel,
        grid=(rows // IN_ROWS,),
        in_specs=[
            pl.BlockSpec((IN_ROWS, d), lambda i: (i, 0)),
            pl.BlockSpec((1, d), lambda i: (0, 0)),
            pl.BlockSpec((d, IN_WIDTH), lambda i: (0, 0), pipeline_mode=pl.Buffered(1)),
            tab, tab, tab,
        ],
        out_specs=[
            pl.BlockSpec((n_planes, IN_ROWS, HEAD_DIM), lambda i: (0, i, 0)),
            pl.BlockSpec((n_planes, None, MAX_DIL, half_rows, HEAD_DIM),
                         lambda i: (0, i // tiles_per_unit, 0, i % tiles_per_unit, 0)),
            pl.BlockSpec((IN_ROWS, 2 * LRU_WIDTH), lambda i: (i, 0)),
        ],
        out_shape=[
            jax.ShapeDtypeStruct((n_planes, rows, HEAD_DIM), BF16),
            jax.ShapeDtypeStruct((n_planes, rows // UNIT, MAX_DIL, CLASS_ROWS, HEAD_DIM), BF16),
            jax.ShapeDtypeStruct((rows, 2 * LRU_WIDTH), F32),
        ],
        scratch_shapes=[pltpu.VMEM((IN_ROWS, d), BF16), pltpu.VMEM((2 * IN_ROWS, HEAD_DIM), F32)],
        compiler_params=_params(("parallel",)),
        name="inproj",
    )(x, g_mix, w_in, cos_t, sa_t, sb_t)
    return nat, cm.reshape(n_planes, rows, HEAD_DIM), rg


def _attn_kernel(qn_ref, knp_ref, knm_ref, knn_ref, vnp_ref, vnm_ref, vnn_ref,
                 qc_ref, kcp_ref, kcm_ref, kcn_ref, vcp_ref, vcm_ref, vcn_ref,
                 out_ref, n1_ref, m1_ref, l1_ref, num_ref, max_ref, den_ref, *, units_per_seq):
    u = pl.program_id(1) % units_per_seq
    has_prev = u > 0
    has_next = u < units_per_seq - 1
    side = KEYS_PER_SIDE
    cat = lambda pieces: jnp.concatenate(pieces, axis=0)
    tiles = []

    n_tiles = UNIT // TQ1
    nk1 = TQ1 + 2 * side
    row = lax.broadcasted_iota(jnp.int32, (TQ1, nk1), 0)
    col = lax.broadcasted_iota(jnp.int32, (TQ1, nk1), 1)
    band = jnp.abs(col - side - row) <= side

    def tile1(c):
        lo, hi = c * TQ1 - side, c * TQ1 + TQ1 + side
        rows_c = slice(c * TQ1, (c + 1) * TQ1)
        if c == 0:
            kv = lambda p, m, n: cat([p[...], m[0:hi, :]])
            mask = band & ((col >= side) | has_prev)
        elif c == n_tiles - 1:
            kv = lambda p, m, n: cat([m[lo:UNIT, :], n[...]])
            mask = band & ((col < TQ1 + side) | has_next)
        else:
            kv = lambda p, m, n: m[lo:hi, :]
            mask = band

        def finish(num, m, l):
            n1_ref[rows_c, :] = num
            m1_ref[rows_c, :] = jnp.broadcast_to(m, (TQ1, HEAD_DIM))
            l1_ref[rows_c, :] = jnp.broadcast_to(l, (TQ1, HEAD_DIM))

        return (lambda: qn_ref[rows_c, :], lambda: kv(knp_ref, knm_ref, knn_ref),
                lambda: kv(vnp_ref, vnm_ref, vnn_ref), mask, finish)

    tiles += [tile1(c) for c in range(n_tiles)]

    nk16 = 3 * CLASS_ROWS
    row = lax.broadcasted_iota(jnp.int32, (CLASS_ROWS, nk16), 0)
    col = lax.broadcasted_iota(jnp.int32, (CLASS_ROWS, nk16), 1)
    mask16 = ((jnp.abs(col - CLASS_ROWS - row) <= side) & ((col >= CLASS_ROWS) | has_prev)
              & ((col < 2 * CLASS_ROWS) | has_next))

    def tile16(r):
        sl = slice(r * CLASS_ROWS, (r + 1) * CLASS_ROWS)
        strided = pl.ds(r, CLASS_ROWS, stride=MAX_DIL)

        def finish(num, m, l):
            m1 = m1_ref[strided, :]
            m_new = jnp.maximum(m1, m)
            e1 = jnp.exp(m1 - m_new)
            e = jnp.exp(m - m_new)
            num_ref[sl, :] = e1 * n1_ref[strided, :] + e * num
            den_ref[sl, :] = e1 * l1_ref[strided, :] + e * l
            max_ref[sl, :] = m_new

        return (lambda: qc_ref[sl, :], lambda: cat([kcp_ref[sl, :], kcm_ref[sl, :], kcn_ref[sl, :]]),
                lambda: cat([vcp_ref[sl, :], vcm_ref[sl, :], vcn_ref[sl, :]]), mask16, finish)

    tiles += [tile16(r) for r in range(MAX_DIL)]

    edge = side // 4
    nq4 = 4 * CLASS_ROWS
    nk4 = nq4 + 2 * side
    row = lax.broadcasted_iota(jnp.int32, (nq4, nk4), 0)
    col = lax.broadcasted_iota(jnp.int32, (nq4, nk4), 1)
    q_idx = 4 * (row & (CLASS_ROWS - 1)) + (row >> 6)
    c_mid = col - side
    c_next = col - side - nq4
    k_idx = jnp.where(
        col < side, 4 * (CLASS_ROWS - edge + (col & (edge - 1))) + (col >> 4) - nq4,
        jnp.where(col < side + nq4, 4 * (c_mid & (CLASS_ROWS - 1)) + (c_mid >> 6),
                  4 * (c_next & (edge - 1)) + (c_next >> 4) + nq4))
    mask4 = ((jnp.abs(k_idx - q_idx) <= side) & ((col >= side) | has_prev)
             & ((col < side + nq4) | has_next))
    def tile4(r4):
        slabs = [slice((4 * s + r4) * CLASS_ROWS, (4 * s + r4 + 1) * CLASS_ROWS) for s in range(4)]

        def gather(ref, part):
            lo, hi = {"tail": (CLASS_ROWS - edge, CLASS_ROWS), "all": (0, CLASS_ROWS), "head": (0, edge)}[part]
            return [ref[sl.start + lo:sl.start + hi, :] for sl in slabs]

        def kv(p, m, n):
            return cat(gather(p, "tail") + gather(m, "all") + gather(n, "head"))

        def finish(num, m, l):
            m_old = cat([max_ref[sl, :] for sl in slabs])
            m_new = jnp.maximum(m_old, m)
            e_old = jnp.exp(m_old - m_new)
            e = jnp.exp(m - m_new)
            res = ((e_old * cat([num_ref[sl, :] for sl in slabs]) + e * num)
                   / (e_old * cat([den_ref[sl, :] for sl in slabs]) + e * l))
            for s in range(4):
                out_ref[pl.ds(4 * s + r4, CLASS_ROWS, stride=MAX_DIL), :] = res[CLASS_ROWS * s:CLASS_ROWS * (s + 1), :]

        return (lambda: cat(gather(qc_ref, "all")), lambda: kv(kcp_ref, kcm_ref, kcn_ref),
                lambda: kv(vcp_ref, vcm_ref, vcn_ref), mask4, finish)

    tiles += [tile4(r4) for r4 in range(4)]

    def scores(tile):
        return lax.dot_general(tile[0](), tile[1](), (((1,), (1,)), ((), ())), preferred_element_type=F32)

    pending = [scores(t) for t in tiles[:ATTN_SKEW]]
    for idx, (_, _, v, mask, finish) in enumerate(tiles):
        s = pending.pop(0)
        if idx + ATTN_SKEW < len(tiles):
            pending.append(scores(tiles[idx + ATTN_SKEW]))
        s = jnp.where(mask, s, NEG_INF)
        m = jnp.max(s, axis=-1, keepdims=True)
        p = jnp.exp(s - m)
        l = jnp.sum(p, axis=-1, keepdims=True)
        finish(jnp.dot(p.astype(BF16), v(), preferred_element_type=F32), m, l)


def _attention(nat, cm, seq):
    rows = nat.shape[1]
    units_per_seq = seq // UNIT
    ratio = UNIT // KEYS_PER_SIDE
    n_side_blocks = rows // KEYS_PER_SIDE
    n_units = rows // UNIT

    def unit(plane, shift):
        return pl.BlockSpec((None, UNIT, HEAD_DIM),
                            lambda h, u: (plane * N_HEADS + h, jnp.clip(u + shift, 0, n_units - 1), 0))

    def halo(plane, after):
        if after:
            index = lambda h, u: (plane * N_HEADS + h, jnp.minimum((u + 1) * ratio, n_side_blocks - 1), 0)
        else:
            index = lambda h, u: (plane * N_HEADS + h, jnp.maximum(u * ratio - 1, 0), 0)
        return pl.BlockSpec((None, KEYS_PER_SIDE, HEAD_DIM), index)

    nat_specs = [unit(0, 0), halo(1, False), unit(1, 0), halo(1, True), halo(2, False), unit(2, 0), halo(2, True)]
    cm_specs = [unit(0, 0), unit(1, -1), unit(1, 0), unit(1, 1), unit(2, -1), unit(2, 0), unit(2, 1)]
    return pl.pallas_call(
        functools.partial(_attn_kernel, units_per_seq=units_per_seq),
        grid=(N_HEADS, n_units),
        in_specs=nat_specs + cm_specs,
        out_specs=pl.BlockSpec((UNIT, HEAD_DIM), lambda h, u: (u, h)),
        out_shape=jax.ShapeDtypeStruct((rows, ATTN_WIDTH), F32),
        scratch_shapes=[pltpu.VMEM((UNIT, HEAD_DIM), F32)] * 6,
        compiler_params=_params(("parallel", "parallel")),
        name="attention",
    )(*([nat] * 7), *([cm] * 7))


def _lru_kernel(*refs, tt, n_tiles, tiles_per_seq, reverse, final):
    if final:
        (xp_ref, xm_ref, xn_ref, gate_ref, other_ref, cw_ref, cb_ref, wax_ref, bax_ref, lamc_ref,
         out_ref, a_ref, b_ref, carry_ref, xs_ref, h_ref) = refs
    else:
        (xp_ref, xm_ref, xn_ref, cw_ref, cb_ref, wax_ref, bax_ref, lamc_ref,
         out_ref, a_ref, b_ref, carry_ref, xs_ref) = refs
        h_ref = out_ref
    i = pl.program_id(0)
    t = ((n_tiles - 1 - i) if reverse else i) % tiles_per_seq

    xs_ref[0:SUBLANES, :] = jnp.where(t == 0, 0.0, xp_ref[...])
    xs_ref[SUBLANES:SUBLANES + tt, :] = xm_ref[...]
    xs_ref[SUBLANES + tt:2 * SUBLANES + tt, :] = jnp.where(t == tiles_per_seq - 1, 0.0, xn_ref[...])
    u = cb_ref[...]
    for j in range(4):
        u = u + xs_ref[pl.ds(SUBLANES - 2 + j, tt), :] * cw_ref[j:j + 1, :]
    ub = u.astype(BF16)

    for g in range(N_LRU_BLOCKS):
        sl = slice(g * LRU_BLOCK, (g + 1) * LRU_BLOCK)
        pre = jnp.dot(ub[:, sl], wax_ref[g], preferred_element_type=F32) + bax_ref[g]
        gates = 0.5 + 0.5 * jnp.tanh(0.5 * pre)
        r = gates[:, :LRU_BLOCK]
        ig = gates[:, LRU_BLOCK:]
        log_a = lamc_ref[:, sl] * r
        a = jnp.exp(log_a)
        a_ref[:, sl] = a
        b_ref[:, sl] = jnp.sqrt(jnp.tanh(-log_a) * (1.0 + a * a)) * (ig * u[:, sl])

    n_groups = tt // SUBLANES
    width = a_ref.shape[1]
    rows = lax.broadcasted_iota(jnp.int32, (SUBLANES, width), 0)
    starts_seq = (t == tiles_per_seq - 1) if reverse else (t == 0)
    carry0 = jnp.where(starts_seq, 0.0, carry_ref[...])

    def body(gi, carry):
        g = (n_groups - 1 - gi) if reverse else gi
        off = pl.multiple_of(g * SUBLANES, SUBLANES)
        a = a_ref[pl.ds(off, SUBLANES), :]
        b = b_ref[pl.ds(off, SUBLANES), :]
        for d in (1, 2, 4):
            shift = (SUBLANES - d) if reverse else d
            keep = (rows < SUBLANES - d) if reverse else (rows >= d)
            a_s = pltpu.roll(a, shift, 0)
            b_s = pltpu.roll(b, shift, 0)
            b = jnp.where(keep, a * b_s + b, b)
            a = jnp.where(keep, a * a_s, a)
        h = a * carry + b
        h_ref[pl.ds(off, SUBLANES), :] = h
        edge = h[0:1, :] if reverse else h[SUBLANES - 1:SUBLANES, :]
        return jnp.broadcast_to(edge, (SUBLANES, width))

    carry_ref[...] = lax.fori_loop(0, n_groups, body, carry0)

    if final:
        out_ref[...] = (h_ref[...] + other_ref[...]) * _gelu_tanh(gate_ref[...])


def _lru_pass(rg, other, conv_w, conv_b, wax, bax, lamc, seq, tt, reverse):
    rows = rg.shape[0]
    tt = min(tt, seq)
    n_tiles = rows // tt
    tiles_per_seq = seq // tt
    ratio = tt // SUBLANES
    n_halo_blocks = rows // SUBLANES
    final = other is not None

    def tile(i):
        return (n_tiles - 1 - i) if reverse else i

    x_specs = [
        pl.BlockSpec((SUBLANES, LRU_WIDTH), lambda i: (jnp.maximum(tile(i) * ratio - 1, 0), 0)),
        pl.BlockSpec((tt, LRU_WIDTH), lambda i: (tile(i), 0)),
        pl.BlockSpec((SUBLANES, LRU_WIDTH),
                     lambda i: (jnp.minimum((tile(i) + 1) * ratio, n_halo_blocks - 1), 0)),
    ]
    w_specs = [
        pl.BlockSpec((4, LRU_WIDTH), lambda i: (0, 0)),
        pl.BlockSpec((1, LRU_WIDTH), lambda i: (0, 0)),
        pl.BlockSpec((N_LRU_BLOCKS, LRU_BLOCK, 2 * LRU_BLOCK), lambda i: (0, 0, 0)),
        pl.BlockSpec((N_LRU_BLOCKS, 1, 2 * LRU_BLOCK), lambda i: (0, 0, 0)),
        pl.BlockSpec((1, LRU_WIDTH), lambda i: (0, 0)),
    ]
    scratch = [pltpu.VMEM((tt, LRU_WIDTH), F32), pltpu.VMEM((tt, LRU_WIDTH), F32),
               pltpu.VMEM((SUBLANES, LRU_WIDTH), F32), pltpu.VMEM((tt + 2 * SUBLANES, LRU_WIDTH), F32)]
    args = [rg, rg, rg]
    if final:
        x_specs += [pl.BlockSpec((tt, LRU_WIDTH), lambda i: (tile(i), 1)),
                    pl.BlockSpec((tt, LRU_WIDTH), lambda i: (tile(i), 0))]
        args += [rg, other]
        scratch.append(pltpu.VMEM((tt, LRU_WIDTH), F32))
    return pl.pallas_call(
        functools.partial(_lru_kernel, tt=tt, n_tiles=n_tiles, tiles_per_seq=tiles_per_seq,
                          reverse=reverse, final=final),
        grid=(n_tiles,),
        in_specs=x_specs + w_specs,
        out_specs=pl.BlockSpec((tt, LRU_WIDTH), lambda i: (tile(i), 0)),
        out_shape=jax.ShapeDtypeStruct((rows, LRU_WIDTH), F32),
        scratch_shapes=scratch,
        compiler_params=_params(("arbitrary",)),
        name="lru_bwd" if reverse else "lru_fwd",
    )(*args, conv_w, conv_b, wax, bax, lamc)


def _outproj_kernel(x_ref, attn_ref, lru_ref, ga_ref, gl_ref, w_ref, out_ref):
    mixed = jnp.concatenate([_rms(attn_ref[...], ga_ref[...]).astype(BF16),
                             _rms(lru_ref[...], gl_ref[...]).astype(BF16)], axis=1)
    out_ref[...] = x_ref[...] + jnp.dot(mixed, w_ref[...], preferred_element_type=F32)


def _outproj(x, attn, lru, g_attn, g_lru, w_out, tm):
    rows, d = x.shape
    row_block = lambda width: pl.BlockSpec((tm, width), lambda i: (i, 0))
    const = lambda shape: pl.BlockSpec(shape, lambda i: (0, 0))
    return pl.pallas_call(
        _outproj_kernel,
        grid=(rows // tm,),
        in_specs=[row_block(d), row_block(ATTN_WIDTH), row_block(LRU_WIDTH),
                  const((1, ATTN_WIDTH)), const((1, LRU_WIDTH)), const(w_out.shape)],
        out_specs=row_block(d),
        out_shape=jax.ShapeDtypeStruct((rows, d), F32),
        compiler_params=_params(("parallel",)),
        name="outproj",
    )(x, attn, lru, g_attn, g_lru, w_out)


def _mlp_kernel(xp_ref, xm_ref, xn_ref, g_ref, wg_ref, wv_ref, cwg_ref, cwv_ref, cbg_ref, cbv_ref,
                wd_ref, gf_ref, out_ref, h_ref, *, tm, tiles_per_seq):
    i = pl.program_id(0)
    k = pl.program_id(1)
    t = i % tiles_per_seq
    n = tm + BF16_ROWS

    @pl.when(k == 0)
    def _():
        g = g_ref[...]
        h_ref[0:tm, :] = _rms(xm_ref[...], g).astype(BF16)
        after = jnp.where(t == tiles_per_seq - 1, 0.0, _rms(xn_ref[...], g))
        before = jnp.where(t == 0, 0.0, _rms(xp_ref[...], g))
        h_ref[tm:n, :] = jnp.concatenate([after, before], axis=0).astype(BF16)
        out_ref[...] = xm_ref[...]

    h = h_ref[...]

    def conv(w_ref, cw_ref, cb_ref):
        u = jnp.dot(h, w_ref[...], preferred_element_type=F32)
        c = (cb_ref[...] + pltpu.roll(u, 1, 0) * cw_ref[0:1, :] + u * cw_ref[1:2, :]
             + pltpu.roll(u, n - 1, 0) * cw_ref[2:3, :])
        return c[0:tm, :]

    act = (_gelu_tanh(conv(wg_ref, cwg_ref, cbg_ref)) * conv(wv_ref, cwv_ref, cbv_ref)).astype(BF16)
    out_ref[...] += jnp.dot(act, wd_ref[...], preferred_element_type=F32)

    @pl.when(k == pl.num_programs(1) - 1)
    def _():
        out_ref[...] = _rms(out_ref[...], gf_ref[...])


def _mlp(x, g_mlp, w_up, conv_w, conv_b, w_down, g_final, seq, tm, tf):
    rows, d = x.shape
    d_ff = w_down.shape[0]
    tm = min(tm, seq)
    tf = min(tf, d_ff)
    nk = d_ff // tf
    tiles_per_seq = seq // tm
    ratio = tm // SUBLANES
    n_halo_blocks = rows // SUBLANES
    return pl.pallas_call(
        functools.partial(_mlp_kernel, tm=tm, tiles_per_seq=tiles_per_seq),
        grid=(rows // tm, nk),
        in_specs=[
            pl.BlockSpec((SUBLANES, d), lambda i, k: (jnp.maximum(i * ratio - 1, 0), 0)),
            pl.BlockSpec((tm, d), lambda i, k: (i, 0)),
            pl.BlockSpec((SUBLANES, d), lambda i, k: (jnp.minimum((i + 1) * ratio, n_halo_blocks - 1), 0)),
            pl.BlockSpec((1, d), lambda i, k: (0, 0)),
            pl.BlockSpec((d, tf), lambda i, k: (0, k)),
            pl.BlockSpec((d, tf), lambda i, k: (0, nk + k)),
            pl.BlockSpec((3, tf), lambda i, k: (0, k)),
            pl.BlockSpec((3, tf), lambda i, k: (0, nk + k)),
            pl.BlockSpec((1, tf), lambda i, k: (0, k)),
            pl.BlockSpec((1, tf), lambda i, k: (0, nk + k)),
            pl.BlockSpec((tf, d), lambda i, k: (k, 0)),
            pl.BlockSpec((1, d), lambda i, k: (0, 0)),
        ],
        out_specs=pl.BlockSpec((tm, d), lambda i, k: (i, 0)),
        out_shape=jax.ShapeDtypeStruct((rows, d), F32),
        scratch_shapes=[pltpu.VMEM((tm + BF16_ROWS, d), BF16)],
        compiler_params=_params(("parallel", "arbitrary")),
        name="mlp",
    )(x, x, x, g_mlp, w_up, w_up, conv_w, conv_w, conv_b, conv_b, w_down, g_final)


def _prepare(g_mix, w_in, w_out, g_attn_out, g_lru_out, conv_rg_w, conv_rg_b, rg_w_a, rg_b_a,
             rg_w_x, rg_b_x, rg_lam, g_mlp, w_up, conv_ff_w, conv_ff_b, w_down, g_final):
    l = 0
    row = lambda v: v.reshape(1, -1).astype(F32)
    wax = jnp.concatenate([rg_w_a[l], rg_w_x[l]], axis=-1).astype(BF16)
    bax = jnp.concatenate([rg_b_a[l].reshape(2, N_LRU_BLOCKS, 1, LRU_BLOCK),
                           rg_b_x[l].reshape(2, N_LRU_BLOCKS, 1, LRU_BLOCK)], axis=-1).astype(F32)
    lamc = (-LRU_C * jax.nn.softplus(-rg_lam[l].astype(F32))).reshape(2, 1, LRU_WIDTH)
    return dict(
        g_mix=row(g_mix[l]), w_in=w_in[l].astype(BF16), w_out=w_out[l].astype(BF16),
        g_attn=row(g_attn_out[l]), g_lru=row(g_lru_out[l]),
        conv_rg_w=conv_rg_w[l].astype(F32), conv_rg_b=row(conv_rg_b[l]),
        wax=wax, bax=bax, lamc=lamc,
        g_mlp=row(g_mlp[l]), w_up=w_up[l].astype(BF16), conv_ff_w=conv_ff_w[l].astype(F32),
        conv_ff_b=row(conv_ff_b[l]), w_down=w_down[l].astype(BF16), g_final=row(g_final),
    )


def _encode(x, p, *, tt=512, tm_out=256, tm_mlp=512, tf=1024):
    batch, seq, d = x.shape
    x2 = x.reshape(batch * seq, d)
    nat, cm, rg = _inproj(x2, p["g_mix"], p["w_in"], seq)
    attn = _attention(nat, cm, seq)
    h_rev = _lru_pass(rg, None, p["conv_rg_w"], p["conv_rg_b"], p["wax"][1], p["bax"][1],
                      p["lamc"][1], seq, tt, reverse=True)
    lru = _lru_pass(rg, h_rev, p["conv_rg_w"], p["conv_rg_b"], p["wax"][0], p["bax"][0],
                    p["lamc"][0], seq, tt, reverse=False)
    x1 = _outproj(x2, attn, lru, p["g_attn"], p["g_lru"], p["w_out"], min(tm_out, seq))
    y = _mlp(x1, p["g_mlp"], p["w_up"], p["conv_ff_w"], p["conv_ff_b"], p["w_down"], p["g_final"],
             seq, tm_mlp, tf)
    return y.reshape(batch, seq, d)


def kernel(x_prompt, x_sample, g_mix, w_in, w_out, g_attn_out, g_lru_out, conv_rg_w, conv_rg_b,
           rg_w_a, rg_b_a, rg_w_x, rg_b_x, rg_lam, g_mlp, w_up, conv_ff_w, conv_ff_b, w_down, g_final):
    p = _prepare(g_mix, w_in, w_out, g_attn_out, g_lru_out, conv_rg_w, conv_rg_b, rg_w_a, rg_b_a,
                 rg_w_x, rg_b_x, rg_lam, g_mlp, w_up, conv_ff_w, conv_ff_b, w_down, g_final)
    return (_encode(x_prompt, p), _encode(x_sample, p))
```

```python
import functools
import math

import jax
import jax.numpy as jnp
from jax import lax
from jax.experimental import pallas as pl
from jax.experimental.pallas import tpu as pltpu

F32 = jnp.float32
BF16 = jnp.bfloat16

HEAD_DIM = 128
N_HEADS = 8
ATTN_WIDTH = N_HEADS * HEAD_DIM
LRU_WIDTH = 1024
N_LRU_BLOCKS = 8
LRU_BLOCK = LRU_WIDTH // N_LRU_BLOCKS
IN_WIDTH = 3 * ATTN_WIDTH + 2 * LRU_WIDTH
MAX_DIL = 16
KEYS_PER_SIDE = 64
UNIT = MAX_DIL * KEYS_PER_SIDE
CLASS_ROWS = UNIT // MAX_DIL
TQ1 = 128
ATTN_SKEW = 4
IN_ROWS = 512
IN_CHUNK = 256
ROPE_THETA = 500000.0
ROPE_HALF = HEAD_DIM // 8
LRU_C = 8.0
NORM_EPS = 1e-6
NEG_INF = -1e30

SUBLANES = 8
BF16_ROWS = 16
VMEM_LIMIT = 56 * 1024 * 1024


def _params(semantics):
    return pltpu.CompilerParams(dimension_semantics=semantics, vmem_limit_bytes=VMEM_LIMIT)


def _rms(x, g):
    return x * lax.rsqrt(jnp.mean(x * x, axis=-1, keepdims=True) + NORM_EPS) * g


def _gelu_tanh(x):
    return x * (0.5 * (1.0 + jnp.tanh(math.sqrt(2.0 / math.pi) * (x + 0.044715 * (x * x * x)))))


def _inproj_kernel(x_ref, g_ref, w_ref, cos_ref, sa_ref, sb_ref, nat_ref, cm_ref, rg_ref, h_ref, z_ref):
    h_ref[...] = _rms(x_ref[...], g_ref[...]).astype(BF16)

    def project(c):
        cols = slice(c * IN_CHUNK, (c + 1) * IN_CHUNK)
        return jnp.dot(h_ref[...], w_ref[:, cols], preferred_element_type=F32)

    heads_per_chunk = IN_CHUNK // HEAD_DIM
    n_chunks = IN_WIDTH // IN_CHUNK
    n_qkv_chunks = 3 * ATTN_WIDTH // IN_CHUNK
    half_rows = IN_ROWS // MAX_DIL
    scale = HEAD_DIM ** -0.5
    rotary = {0: (cos_ref[...] * scale, sa_ref[...] * scale, sb_ref[...] * scale),
              1: (cos_ref[...], sa_ref[...], sb_ref[...])}

    z_next = project(0)
    for c in range(n_chunks):
        z = z_next
        if c + 1 < n_chunks:
            z_next = project(c + 1)
        if c >= n_qkv_chunks:
            off = (c - n_qkv_chunks) * IN_CHUNK
            rg_ref[:, off:off + IN_CHUNK] = z
            continue
        for hc in range(heads_per_chunk):
            plane = c * heads_per_chunk + hc
            t = z[:, hc * HEAD_DIM:(hc + 1) * HEAD_DIM]
            if plane // N_HEADS in rotary:
                cos, sa, sb = rotary[plane // N_HEADS]
                t = t * cos + pltpu.roll(t, HEAD_DIM - ROPE_HALF, 1) * sa + pltpu.roll(t, ROPE_HALF, 1) * sb
            nat_ref[plane] = t.astype(BF16)
            slot = plane % 2
            z_ref[slot * IN_ROWS:(slot + 1) * IN_ROWS, :] = t
            for r in range(MAX_DIL):
                piece = z_ref[pl.ds(slot * IN_ROWS + r, half_rows, stride=MAX_DIL), :]
                cm_ref[plane, r] = piece.astype(BF16)


def _rope_tables(seq):
    inv = ROPE_THETA ** (-jnp.arange(ROPE_HALF, dtype=F32) / ROPE_HALF)
    ang = jnp.arange(seq).astype(F32)[:, None] * inv[None, :]
    cos, sin = jnp.cos(ang), jnp.sin(ang)
    rest = HEAD_DIM - 2 * ROPE_HALF
    cos_t = jnp.concatenate([cos, cos, jnp.ones((seq, rest), F32)], axis=1)
    sa_t = jnp.concatenate([-sin, jnp.zeros((seq, HEAD_DIM - ROPE_HALF), F32)], axis=1)
    sb_t = jnp.concatenate([jnp.zeros((seq, ROPE_HALF), F32), sin, jnp.zeros((seq, rest), F32)], axis=1)
    return cos_t, sa_t, sb_t


def _inproj(x, g_mix, w_in, seq):
    rows, d = x.shape
    tiles_per_seq = seq // IN_ROWS
    tiles_per_unit = UNIT // IN_ROWS
    half_rows = IN_ROWS // MAX_DIL
    cos_t, sa_t, sb_t = _rope_tables(seq)
    tab = pl.BlockSpec((IN_ROWS, HEAD_DIM), lambda i: (i % tiles_per_seq, 0))
    n_planes = 3 * N_HEADS
    nat, cm, rg = pl.pallas_call(
        _inproj_kernel,
        grid=(rows // IN_ROWS,),
        in_specs=[
            pl.BlockSpec((IN_ROWS, d), lambda i: (i, 0)),
            pl.BlockSpec((1, d), lambda i: (0, 0)),
            pl.BlockSpec((d, IN_WIDTH), lambda i: (0, 0), pipeline_mode=pl.Buffered(1)),
            tab, tab, tab,
        ],
        out_specs=[
            pl.BlockSpec((n_planes, IN_ROWS, HEAD_DIM), lambda i: (0, i, 0)),
            pl.BlockSpec((n_planes, None, MAX_DIL, half_rows, HEAD_DIM),
                         lambda i: (0, i // tiles_per_unit, 0, i % tiles_per_unit, 0)),
            pl.BlockSpec((IN_ROWS, 2 * LRU_WIDTH), lambda i: (i, 0)),
        ],
        out_shape=[
            jax.ShapeDtypeStruct((n_planes, rows, HEAD_DIM), BF16),
            jax.ShapeDtypeStruct((n_planes, rows // UNIT, MAX_DIL, CLASS_ROWS, HEAD_DIM), BF16),
            jax.ShapeDtypeStruct((rows, 2 * LRU_WIDTH), F32),
        ],
        scratch_shapes=[pltpu.VMEM((IN_ROWS, d), BF16), pltpu.VMEM((2 * IN_ROWS, HEAD_DIM), F32)],
        compiler_params=_params(("parallel",)),
        name="inproj",
    )(x, g_mix, w_in, cos_t, sa_t, sb_t)
    return nat, cm.reshape(n_planes, rows, HEAD_DIM), rg


def _attn_kernel(qn_ref, knp_ref, knm_ref, knn_ref, vnp_ref, vnm_ref, vnn_ref,
                 qc_ref, kcp_ref, kcm_ref, kcn_ref, vcp_ref, vcm_ref, vcn_ref,
                 out_ref, n1_ref, m1_ref, l1_ref, num_ref, max_ref, den_ref, *, units_per_seq):
    u = pl.program_id(1) % units_per_seq
    has_prev = u > 0
    has_next = u < units_per_seq - 1
    side = KEYS_PER_SIDE
    cat = lambda pieces: jnp.concatenate(pieces, axis=0)
    tiles = []

    n_tiles = UNIT // TQ1
    nk1 = TQ1 + 2 * side
    row = lax.broadcasted_iota(jnp.int32, (TQ1, nk1), 0)
    col = lax.broadcasted_iota(jnp.int32, (TQ1, nk1), 1)
    band = jnp.abs(col - side - row) <= side

    def tile1(c):
        lo, hi = c * TQ1 - side, c * TQ1 + TQ1 + side
        rows_c = slice(c * TQ1, (c + 1) * TQ1)
        if c == 0:
            kv = lambda p, m, n: cat([p[...], m[0:hi, :]])
            mask = band & ((col >= side) | has_prev)
        elif c == n_tiles - 1:
            kv = lambda p, m, n: cat([m[lo:UNIT, :], n[...]])
            mask = band & ((col < TQ1 + side) | has_next)
        else:
            kv = lambda p, m, n: m[lo:hi, :]
            mask = band

        def finish(num, m, l):
            n1_ref[rows_c, :] = num
            m1_ref[rows_c, :] = jnp.broadcast_to(m, (TQ1, HEAD_DIM))
            l1_ref[rows_c, :] = jnp.broadcast_to(l, (TQ1, HEAD_DIM))

        return (lambda: qn_ref[rows_c, :], lambda: kv(knp_ref, knm_ref, knn_ref),
                lambda: kv(vnp_ref, vnm_ref, vnn_ref), mask, finish)

    tiles += [tile1(c) for c in range(n_tiles)]

    nk16 = 3 * CLASS_ROWS
    row = lax.broadcasted_iota(jnp.int32, (CLASS_ROWS, nk16), 0)
    col = lax.broadcasted_iota(jnp.int32, (CLASS_ROWS, nk16), 1)
    mask16 = ((jnp.abs(col - CLASS_ROWS - row) <= side) & ((col >= CLASS_ROWS) | has_prev)
              & ((col < 2 * CLASS_ROWS) | has_next))

    def tile16(r):
        sl = slice(r * CLASS_ROWS, (r + 1) * CLASS_ROWS)
        strided = pl.ds(r, CLASS_ROWS, stride=MAX_DIL)

        def finish(num, m, l):
            m1 = m1_ref[strided, :]
            m_new = jnp.maximum(m1, m)
            e1 = jnp.exp(m1 - m_new)
            e = jnp.exp(m - m_new)
            num_ref[sl, :] = e1 * n1_ref[strided, :] + e * num
            den_ref[sl, :] = e1 * l1_ref[strided, :] + e * l
            max_ref[sl, :] = m_new

        return (lambda: qc_ref[sl, :], lambda: cat([kcp_ref[sl, :], kcm_ref[sl, :], kcn_ref[sl, :]]),
                lambda: cat([vcp_ref[sl, :], vcm_ref[sl, :], vcn_ref[sl, :]]), mask16, finish)

    tiles += [tile16(r) for r in range(MAX_DIL)]

    edge = side // 4
    nq4 = 4 * CLASS_ROWS
    nk4 = nq4 + 2 * side
    row = lax.broadcasted_iota(jnp.int32, (nq4, nk4), 0)
    col = lax.broadcasted_iota(jnp.int32, (nq4, nk4), 1)
    q_idx = 4 * (row & (CLASS_ROWS - 1)) + (row >> 6)
    c_mid = col - side
    c_next = col - side - nq4
    k_idx = jnp.where(
        col < side, 4 * (CLASS_ROWS - edge + (col & (edge - 1))) + (col >> 4) - nq4,
        jnp.where(col < side + nq4, 4 * (c_mid & (CLASS_ROWS - 1)) + (c_mid >> 6),
                  4 * (c_next & (edge - 1)) + (c_next >> 4) + nq4))
    mask4 = ((jnp.abs(k_idx - q_idx) <= side) & ((col >= side) | has_prev)
             & ((col < side + nq4) | has_next))
    def tile4(r4):
        slabs = [slice((4 * s + r4) * CLASS_ROWS, (4 * s + r4 + 1) * CLASS_ROWS) for s in range(4)]

        def gather(ref, part):
            lo, hi = {"tail": (CLASS_ROWS - edge, CLASS_ROWS), "all": (0, CLASS_ROWS), "head": (0, edge)}[part]
            return [ref[sl.start + lo:sl.start + hi, :] for sl in slabs]

        def kv(p, m, n):
            return cat(gather(p, "tail") + gather(m, "all") + gather(n, "head"))

        def finish(num, m, l):
            m_old = cat([max_ref[sl, :] for sl in slabs])
            m_new = jnp.maximum(m_old, m)
            e_old = jnp.exp(m_old - m_new)
            e = jnp.exp(m - m_new)
            res = ((e_old * cat([num_ref[sl, :] for sl in slabs]) + e * num)
                   / (e_old * cat([den_ref[sl, :] for sl in slabs]) + e * l))
            for s in range(4):
                out_ref[pl.ds(4 * s + r4, CLASS_ROWS, stride=MAX_DIL), :] = res[CLASS_ROWS * s:CLASS_ROWS * (s + 1), :]

        return (lambda: cat(gather(qc_ref, "all")), lambda: kv(kcp_ref, kcm_ref, kcn_ref),
                lambda: kv(vcp_ref, vcm_ref, vcn_ref), mask4, finish)

    tiles += [tile4(r4) for r4 in range(4)]

    def scores(tile):
        return lax.dot_general(tile[0](), tile[1](), (((1,), (1,)), ((), ())), preferred_element_type=F32)

    pending = [scores(t) for t in tiles[:ATTN_SKEW]]
    for idx, (_, _, v, mask, finish) in enumerate(tiles):
        s = pending.pop(0)
        if idx + ATTN_SKEW < len(tiles):
            pending.append(scores(tiles[idx + ATTN_SKEW]))
        s = jnp.where(mask, s, NEG_INF)
        m = jnp.max(s, axis=-1, keepdims=True)
        p = jnp.exp(s - m)
        l = jnp.sum(p, axis=-1, keepdims=True)
        finish(jnp.dot(p.astype(BF16), v(), preferred_element_type=F32), m, l)


def _attention(nat, cm, seq):
    rows = nat.shape[1]
    units_per_seq = seq // UNIT
    ratio = UNIT // KEYS_PER_SIDE
    n_side_blocks = rows // KEYS_PER_SIDE
    n_units = rows // UNIT

    def unit(plane, shift):
        return pl.BlockSpec((None, UNIT, HEAD_DIM),
                            lambda h, u: (plane * N_HEADS + h, jnp.clip(u + shift, 0, n_units - 1), 0))

    def halo(plane, after):
        if after:
            index = lambda h, u: (plane * N_HEADS + h, jnp.minimum((u + 1) * ratio, n_side_blocks - 1), 0)
        else:
            index = lambda h, u: (plane * N_HEADS + h, jnp.maximum(u * ratio - 1, 0), 0)
        return pl.BlockSpec((None, KEYS_PER_SIDE, HEAD_DIM), index)

    nat_specs = [unit(0, 0), halo(1, False), unit(1, 0), halo(1, True), halo(2, False), unit(2, 0), halo(2, True)]
    cm_specs = [unit(0, 0), unit(1, -1), unit(1, 0), unit(1, 1), unit(2, -1), unit(2, 0), unit(2, 1)]
    return pl.pallas_call(
        functools.partial(_attn_kernel, units_per_seq=units_per_seq),
        grid=(N_HEADS, n_units),
        in_specs=nat_specs + cm_specs,
        out_specs=pl.BlockSpec((UNIT, HEAD_DIM), lambda h, u: (u, h)),
        out_shape=jax.ShapeDtypeStruct((rows, ATTN_WIDTH), F32),
        scratch_shapes=[pltpu.VMEM((UNIT, HEAD_DIM), F32)] * 6,
        compiler_params=_params(("parallel", "parallel")),
        name="attention",
    )(*([nat] * 7), *([cm] * 7))


def _lru_tile(xp_ref, xm_ref, xn_ref, cw_ref, cb_ref, wax_ref, bax_ref, lamc_ref,
              a_ref, b_ref, carry_ref, xs_ref, h_ref, *, t, tt, tiles_per_seq, reverse, between=None):
    xs_ref[0:SUBLANES, :] = jnp.where(t == 0, 0.0, xp_ref[...])
    xs_ref[SUBLANES:SUBLANES + tt, :] = xm_ref[...]
    xs_ref[SUBLANES + tt:2 * SUBLANES + tt, :] = jnp.where(t == tiles_per_seq - 1, 0.0, xn_ref[...])
    u = cb_ref[...]
    for j in range(4):
        u = u + xs_ref[pl.ds(SUBLANES - 2 + j, tt), :] * cw_ref[j:j + 1, :]
    ub = u.astype(BF16)

    pres = [jnp.dot(ub[:, g * LRU_BLOCK:(g + 1) * LRU_BLOCK], wax_ref[g], preferred_element_type=F32) + bax_ref[g]
            for g in range(N_LRU_BLOCKS)]
    if between is not None:
        between()

    for g in range(N_LRU_BLOCKS):
        sl = slice(g * LRU_BLOCK, (g + 1) * LRU_BLOCK)
        gates = 0.5 + 0.5 * jnp.tanh(0.5 * pres[g])
        r = gates[:, :LRU_BLOCK]
        ig = gates[:, LRU_BLOCK:]
        log_a = lamc_ref[:, sl] * r
        a = jnp.exp(log_a)
        a_ref[:, sl] = a
        b_ref[:, sl] = jnp.sqrt(jnp.tanh(-log_a) * (1.0 + a * a)) * (ig * u[:, sl])

    n_groups = tt // SUBLANES
    width = a_ref.shape[1]
    rows = lax.broadcasted_iota(jnp.int32, (SUBLANES, width), 0)
    starts_seq = (t == tiles_per_seq - 1) if reverse else (t == 0)
    carry0 = jnp.where(starts_seq, 0.0, carry_ref[...])

    def body(gi, carry):
        g = (n_groups - 1 - gi) if reverse else gi
        off = pl.multiple_of(g * SUBLANES, SUBLANES)
        a = a_ref[pl.ds(off, SUBLANES), :]
        b = b_ref[pl.ds(off, SUBLANES), :]
        for d in (1, 2, 4):
            shift = (SUBLANES - d) if reverse else d
            keep = (rows < SUBLANES - d) if reverse else (rows >= d)
            a_s = pltpu.roll(a, shift, 0)
            b_s = pltpu.roll(b, shift, 0)
            b = jnp.where(keep, a * b_s + b, b)
            a = jnp.where(keep, a * a_s, a)
        h = a * carry + b
        h_ref[pl.ds(off, SUBLANES), :] = h
        edge = h[0:1, :] if reverse else h[SUBLANES - 1:SUBLANES, :]
        return jnp.broadcast_to(edge, (SUBLANES, width))

    carry_ref[...] = lax.fori_loop(0, n_groups, body, carry0)

def _lru_rev_kernel(xp_ref, xm_ref, xn_ref, cw_ref, cb_ref, wax_ref, bax_ref, lamc_ref,
                    out_ref, a_ref, b_ref, carry_ref, xs_ref, *, tt, n_tiles, tiles_per_seq):
    t = (n_tiles - 1 - pl.program_id(0)) % tiles_per_seq
    _lru_tile(xp_ref, xm_ref, xn_ref, cw_ref, cb_ref, wax_ref, bax_ref, lamc_ref,
              a_ref, b_ref, carry_ref, xs_ref, out_ref, t=t, tt=tt, tiles_per_seq=tiles_per_seq, reverse=True)


def _lru_out_kernel(xp_ref, xm_ref, xn_ref, gate_ref, other_ref, x_ref, attn_ref,
                    cw_ref, cb_ref, wax_ref, bax_ref, lamc_ref, ga_ref, gl_ref, w_ref,
                    out_ref, a_ref, b_ref, carry_ref, xs_ref, h_ref, mixed_ref, *, tt, n_tiles, tiles_per_seq):
    i = pl.program_id(0)

    @pl.when(i == 0)
    def _():
        mixed_ref[...] = jnp.zeros_like(mixed_ref)

    mixed = jnp.concatenate([_rms(attn_ref[...], ga_ref[...]).astype(BF16), mixed_ref[...]], axis=1)
    half = out_ref.shape[1] // 2

    def project(cols):
        out_ref[:, cols] = x_ref[:, cols] + jnp.dot(mixed, w_ref[:, cols], preferred_element_type=F32)

    project(slice(0, half))
    t = jnp.minimum(i, n_tiles - 1) % tiles_per_seq
    _lru_tile(xp_ref, xm_ref, xn_ref, cw_ref, cb_ref, wax_ref, bax_ref, lamc_ref,
              a_ref, b_ref, carry_ref, xs_ref, h_ref, t=t, tt=tt, tiles_per_seq=tiles_per_seq, reverse=False,
              between=lambda: project(slice(half, 2 * half)))
    lru = (h_ref[...] + other_ref[...]) * _gelu_tanh(gate_ref[...])
    mixed_ref[...] = _rms(lru, gl_ref[...]).astype(BF16)


def _lru_specs(rows, tt, tile):
    ratio = tt // SUBLANES
    n_halo_blocks = rows // SUBLANES
    x_specs = [
        pl.BlockSpec((SUBLANES, LRU_WIDTH), lambda i: (jnp.maximum(tile(i) * ratio - 1, 0), 0)),
        pl.BlockSpec((tt, LRU_WIDTH), lambda i: (tile(i), 0)),
        pl.BlockSpec((SUBLANES, LRU_WIDTH),
                     lambda i: (jnp.minimum((tile(i) + 1) * ratio, n_halo_blocks - 1), 0)),
    ]
    w_specs = [
        pl.BlockSpec((4, LRU_WIDTH), lambda i: (0, 0)),
        pl.BlockSpec((1, LRU_WIDTH), lambda i: (0, 0)),
        pl.BlockSpec((N_LRU_BLOCKS, LRU_BLOCK, 2 * LRU_BLOCK), lambda i: (0, 0, 0)),
        pl.BlockSpec((N_LRU_BLOCKS, 1, 2 * LRU_BLOCK), lambda i: (0, 0, 0)),
        pl.BlockSpec((1, LRU_WIDTH), lambda i: (0, 0)),
    ]
    scratch = [pltpu.VMEM((tt, LRU_WIDTH), F32), pltpu.VMEM((tt, LRU_WIDTH), F32),
               pltpu.VMEM((SUBLANES, LRU_WIDTH), F32), pltpu.VMEM((tt + 2 * SUBLANES, LRU_WIDTH), F32)]
    return x_specs, w_specs, scratch


def _lru_reverse(rg, conv_w, conv_b, wax, bax, lamc, seq, tt):
    rows = rg.shape[0]
    tt = min(tt, seq)
    n_tiles = rows // tt
    tile = lambda i: n_tiles - 1 - i
    x_specs, w_specs, scratch = _lru_specs(rows, tt, tile)
    return pl.pallas_call(
        functools.partial(_lru_rev_kernel, tt=tt, n_tiles=n_tiles, tiles_per_seq=seq // tt),
        grid=(n_tiles,),
        in_specs=x_specs + w_specs,
        out_specs=pl.BlockSpec((tt, LRU_WIDTH), lambda i: (tile(i), 0)),
        out_shape=jax.ShapeDtypeStruct((rows, LRU_WIDTH), F32),
        scratch_shapes=scratch,
        compiler_params=_params(("arbitrary",)),
        name="lru_rev",
    )(rg, rg, rg, conv_w, conv_b, wax, bax, lamc)


def _lru_forward_outproj(x, rg, h_rev, attn, conv_w, conv_b, wax, bax, lamc, g_attn, g_lru, w_out, seq, tt):
    rows, d = x.shape
    tt = min(tt, seq)
    n_tiles = rows // tt
    tile = lambda i: jnp.minimum(i, n_tiles - 1)
    prev = lambda i: jnp.maximum(i - 1, 0)
    x_specs, w_specs, scratch = _lru_specs(rows, tt, tile)
    x_specs += [
        pl.BlockSpec((tt, LRU_WIDTH), lambda i: (tile(i), 1)),
        pl.BlockSpec((tt, LRU_WIDTH), lambda i: (tile(i), 0)),
        pl.BlockSpec((tt, d), lambda i: (prev(i), 0)),
        pl.BlockSpec((tt, ATTN_WIDTH), lambda i: (prev(i), 0)),
    ]
    w_specs += [
        pl.BlockSpec((1, ATTN_WIDTH), lambda i: (0, 0)),
        pl.BlockSpec((1, LRU_WIDTH), lambda i: (0, 0)),
        pl.BlockSpec(w_out.shape, lambda i: (0, 0), pipeline_mode=pl.Buffered(1)),
    ]
    scratch += [pltpu.VMEM((tt, LRU_WIDTH), F32), pltpu.VMEM((tt, LRU_WIDTH), BF16)]
    return pl.pallas_call(
        functools.partial(_lru_out_kernel, tt=tt, n_tiles=n_tiles, tiles_per_seq=seq // tt),
        grid=(n_tiles + 1,),
        in_specs=x_specs + w_specs,
        out_specs=pl.BlockSpec((tt, d), lambda i: (prev(i), 0)),
        out_shape=jax.ShapeDtypeStruct((rows, d), F32),
        scratch_shapes=scratch,
        compiler_params=_params(("arbitrary",)),
        name="lru_fwd_outproj",
    )(rg, rg, rg, rg, h_rev, x, attn, conv_w, conv_b, wax, bax, lamc, g_attn, g_lru, w_out)


def _mlp_kernel(xp_ref, xm_ref, xn_ref, g_ref, wg_ref, wv_ref, cwg_ref, cwv_ref, cbg_ref, cbv_ref,
                wd_ref, gf_ref, out_ref, h_ref, *, tm, tiles_per_seq):
    i = pl.program_id(0)
    k = pl.program_id(1)
    t = i % tiles_per_seq
    n = tm + BF16_ROWS

    @pl.when(k == 0)
    def _():
        g = g_ref[...]
        h_ref[0:tm, :] = _rms(xm_ref[...], g).astype(BF16)
        after = jnp.where(t == tiles_per_seq - 1, 0.0, _rms(xn_ref[...], g))
        before = jnp.where(t == 0, 0.0, _rms(xp_ref[...], g))
        h_ref[tm:n, :] = jnp.concatenate([after, before], axis=0).astype(BF16)
        out_ref[...] = xm_ref[...]

    h = h_ref[...]

    def conv(w_ref, cw_ref, cb_ref):
        u = jnp.dot(h, w_ref[...], preferred_element_type=F32)
        c = (cb_ref[...] + pltpu.roll(u, 1, 0) * cw_ref[0:1, :] + u * cw_ref[1:2, :]
             + pltpu.roll(u, n - 1, 0) * cw_ref[2:3, :])
        return c[0:tm, :]

    act = (_gelu_tanh(conv(wg_ref, cwg_ref, cbg_ref)) * conv(wv_ref, cwv_ref, cbv_ref)).astype(BF16)
    out_ref[...] += jnp.dot(act, wd_ref[...], preferred_element_type=F32)

    @pl.when(k == pl.num_programs(1) - 1)
    def _():
        out_ref[...] = _rms(out_ref[...], gf_ref[...])


def _mlp(x, g_mlp, w_up, conv_w, conv_b, w_down, g_final, seq, tm, tf):
    rows, d = x.shape
    d_ff = w_down.shape[0]
    tm = min(tm, seq)
    tf = min(tf, d_ff)
    nk = d_ff // tf
    tiles_per_seq = seq // tm
    ratio = tm // SUBLANES
    n_halo_blocks = rows // SUBLANES
    return pl.pallas_call(
        functools.partial(_mlp_kernel, tm=tm, tiles_per_seq=tiles_per_seq),
        grid=(rows // tm, nk),
        in_specs=[
            pl.BlockSpec((SUBLANES, d), lambda i, k: (jnp.maximum(i * ratio - 1, 0), 0)),
            pl.BlockSpec((tm, d), lambda i, k: (i, 0)),
            pl.BlockSpec((SUBLANES, d), lambda i, k: (jnp.minimum((i + 1) * ratio, n_halo_blocks - 1), 0)),
            pl.BlockSpec((1, d), lambda i, k: (0, 0)),
            pl.BlockSpec((d, tf), lambda i, k: (0, k)),
            pl.BlockSpec((d, tf), lambda i, k: (0, nk + k)),
            pl.BlockSpec((3, tf), lambda i, k: (0, k)),
            pl.BlockSpec((3, tf), lambda i, k: (0, nk + k)),
            pl.BlockSpec((1, tf), lambda i, k: (0, k)),
            pl.BlockSpec((1, tf), lambda i, k: (0, nk + k)),
            pl.BlockSpec((tf, d), lambda i, k: (k, 0)),
            pl.BlockSpec((1, d), lambda i, k: (0, 0)),
        ],
        out_specs=pl.BlockSpec((tm, d), lambda i, k: (i, 0)),
        out_shape=jax.ShapeDtypeStruct((rows, d), F32),
        scratch_shapes=[pltpu.VMEM((tm + BF16_ROWS, d), BF16)],
        compiler_params=_params(("parallel", "arbitrary")),
        name="mlp",
    )(x, x, x, g_mlp, w_up, w_up, conv_w, conv_w, conv_b, conv_b, w_down, g_final)


def _prepare(g_mix, w_in, w_out, g_attn_out, g_lru_out, conv_rg_w, conv_rg_b, rg_w_a, rg_b_a,
             rg_w_x, rg_b_x, rg_lam, g_mlp, w_up, conv_ff_w, conv_ff_b, w_down, g_final):
    l = 0
    row = lambda v: v.reshape(1, -1).astype(F32)
    wax = jnp.concatenate([rg_w_a[l], rg_w_x[l]], axis=-1).astype(BF16)
    bax = jnp.concatenate([rg_b_a[l].reshape(2, N_LRU_BLOCKS, 1, LRU_BLOCK),
                           rg_b_x[l].reshape(2, N_LRU_BLOCKS, 1, LRU_BLOCK)], axis=-1).astype(F32)
    lamc = (-LRU_C * jax.nn.softplus(-rg_lam[l].astype(F32))).reshape(2, 1, LRU_WIDTH)
    return dict(
        g_mix=row(g_mix[l]), w_in=w_in[l].astype(BF16), w_out=w_out[l].astype(BF16),
        g_attn=row(g_attn_out[l]), g_lru=row(g_lru_out[l]),
        conv_rg_w=conv_rg_w[l].astype(F32), conv_rg_b=row(conv_rg_b[l]),
        wax=wax, bax=bax, lamc=lamc,
        g_mlp=row(g_mlp[l]), w_up=w_up[l].astype(BF16), conv_ff_w=conv_ff_w[l].astype(F32),
        conv_ff_b=row(conv_ff_b[l]), w_down=w_down[l].astype(BF16), g_final=row(g_final),
    )


def _encode(x, p, *, tt=512, tm_mlp=512, tf=1024):
    batch, seq, d = x.shape
    x2 = x.reshape(batch * seq, d)
    nat, cm, rg = _inproj(x2, p["g_mix"], p["w_in"], seq)
    attn = _attention(nat, cm, seq)
    h_rev = _lru_reverse(rg, p["conv_rg_w"], p["conv_rg_b"], p["wax"][1], p["bax"][1], p["lamc"][1], seq, tt)
    x1 = _lru_forward_outproj(x2, rg, h_rev, attn, p["conv_rg_w"], p["conv_rg_b"], p["wax"][0], p["bax"][0],
                              p["lamc"][0], p["g_attn"], p["g_lru"], p["w_out"], seq, tt)
    y = _mlp(x1, p["g_mlp"], p["w_up"], p["conv_ff_w"], p["conv_ff_b"], p["w_down"], p["g_final"],
             seq, tm_mlp, tf)
    return y.reshape(batch, seq, d)


def kernel(x_prompt, x_sample, g_mix, w_in, w_out, g_attn_out, g_lru_out, conv_rg_w, conv_rg_b,
           rg_w_a, rg_b_a, rg_w_x, rg_b_x, rg_lam, g_mlp, w_up, conv_ff_w, conv_ff_b, w_down, g_final):
    p = _prepare(g_mix, w_in, w_out, g_attn_out, g_lru_out, conv_rg_w, conv_rg_b, rg_w_a, rg_b_a,
                 rg_w_x, rg_b_x, rg_lam, g_mlp, w_up, conv_ff_w, conv_ff_b, w_down, g_final)
    return (_encode(x_prompt, p), _encode(x_sample, p))
```

```python
import functools
import math

import jax
import jax.numpy as jnp
from jax import lax
from jax.experimental import pallas as pl
from jax.experimental.pallas import tpu as pltpu

F32 = jnp.float32
BF16 = jnp.bfloat16

HEAD_DIM = 128
N_HEADS = 8
ATTN_WIDTH = N_HEADS * HEAD_DIM
LRU_WIDTH = 1024
N_LRU_BLOCKS = 8
LRU_BLOCK = LRU_WIDTH // N_LRU_BLOCKS
IN_WIDTH = 3 * ATTN_WIDTH + 2 * LRU_WIDTH
MAX_DIL = 16
KEYS_PER_SIDE = 64
UNIT = MAX_DIL * KEYS_PER_SIDE
CLASS_ROWS = UNIT // MAX_DIL
TQ1 = 256
ATTN_SKEW = 4
IN_ROWS = 512
IN_CHUNK = 256
IN_GROUP = 512
ROPE_THETA = 500000.0
ROPE_HALF = HEAD_DIM // 8
LRU_C = 8.0
NORM_EPS = 1e-6
NEG_INF = -1e30

SUBLANES = 8
BF16_ROWS = 16
VMEM_LIMIT = 60 * 1024 * 1024


def _params(semantics):
    return pltpu.CompilerParams(dimension_semantics=semantics, vmem_limit_bytes=VMEM_LIMIT)


def _rms(x, g):
    return x * lax.rsqrt(jnp.mean(x * x, axis=-1, keepdims=True) + NORM_EPS) * g


def _gelu_tanh(x):
    return x * (0.5 * (1.0 + jnp.tanh(math.sqrt(2.0 / math.pi) * (x + 0.044715 * (x * x * x)))))


def _inproj_kernel(x_ref, g_ref, w_ref, perm_ref, class_ref, cos_ref, sa_ref, sb_ref, nat_ref, cm_ref, rg_ref,
                   h_ref, hil_ref, zb_ref):
    h_ref[...] = _rms(x_ref[...], g_ref[...]).astype(BF16)
    hil_ref[...] = jnp.dot(perm_ref[...], h_ref[...], preferred_element_type=F32).astype(BF16)

    heads_per_chunk = IN_CHUNK // HEAD_DIM
    n_chunks = IN_WIDTH // IN_CHUNK
    n_qkv_chunks = 3 * ATTN_WIDTH // IN_CHUNK

    def project(c):
        cols = slice(c * IN_CHUNK, (c + 1) * IN_CHUNK)
        lhs = h_ref if c < n_qkv_chunks else hil_ref
        return jnp.dot(lhs[...], w_ref[:, cols], preferred_element_type=F32)

    half_rows = IN_ROWS // MAX_DIL
    scale = HEAD_DIM ** -0.5
    rotary = {0: (cos_ref[...] * scale, sa_ref[...] * scale, sb_ref[...] * scale),
              1: (cos_ref[...], sa_ref[...], sb_ref[...])}

    z_next = project(0)
    for c in range(n_chunks):
        z = z_next
        if c + 1 < n_chunks:
            z_next = project(c + 1)
        if c >= n_qkv_chunks:
            off = (c - n_qkv_chunks) * IN_CHUNK
            rg_ref[:, off:off + IN_CHUNK] = z
            continue
        for hc in range(heads_per_chunk):
            plane = c * heads_per_chunk + hc
            t = z[:, hc * HEAD_DIM:(hc + 1) * HEAD_DIM]
            if plane // N_HEADS in rotary:
                cos, sa, sb = rotary[plane // N_HEADS]
                t = t * cos + pltpu.roll(t, HEAD_DIM - ROPE_HALF, 1) * sa + pltpu.roll(t, ROPE_HALF, 1) * sb
            t = t.astype(BF16)
            zb_ref[:, plane * HEAD_DIM:(plane + 1) * HEAD_DIM] = t
            if plane >= N_HEADS:
                nat_ref[plane - N_HEADS] = t
        done = (c + 1) * IN_CHUNK
        if done % IN_GROUP == 0:
            cols = slice(done - IN_GROUP, done)
            grouped = jnp.dot(class_ref[...], zb_ref[:, cols], preferred_element_type=F32).astype(BF16)
            for hg in range(IN_GROUP // HEAD_DIM):
                plane = (done - IN_GROUP) // HEAD_DIM + hg
                for r in range(MAX_DIL):
                    cm_ref[plane, r] = grouped[r * half_rows:(r + 1) * half_rows, hg * HEAD_DIM:(hg + 1) * HEAD_DIM]


def _class_major_matrix():
    per_class = IN_ROWS // MAX_DIL
    row = jnp.arange(IN_ROWS)
    position = MAX_DIL * (row % per_class) + row // per_class
    return (position[:, None] == jnp.arange(IN_ROWS)[None, :]).astype(BF16)


def _interleave_matrix():
    n_groups = IN_ROWS // SUBLANES
    row = jnp.arange(IN_ROWS)
    time = n_groups * (row % SUBLANES) + row // SUBLANES
    return (time[:, None] == jnp.arange(IN_ROWS)[None, :]).astype(BF16)


def _rope_tables(seq):
    inv = ROPE_THETA ** (-jnp.arange(ROPE_HALF, dtype=F32) / ROPE_HALF)
    ang = jnp.arange(seq).astype(F32)[:, None] * inv[None, :]
    cos, sin = jnp.cos(ang), jnp.sin(ang)
    rest = HEAD_DIM - 2 * ROPE_HALF
    cos_t = jnp.concatenate([cos, cos, jnp.ones((seq, rest), F32)], axis=1)
    sa_t = jnp.concatenate([-sin, jnp.zeros((seq, HEAD_DIM - ROPE_HALF), F32)], axis=1)
    sb_t = jnp.concatenate([jnp.zeros((seq, ROPE_HALF), F32), sin, jnp.zeros((seq, rest), F32)], axis=1)
    return cos_t, sa_t, sb_t


def _inproj(x, g_mix, w_in, seq):
    rows, d = x.shape
    tiles_per_seq = seq // IN_ROWS
    tiles_per_unit = UNIT // IN_ROWS
    half_rows = IN_ROWS // MAX_DIL
    cos_t, sa_t, sb_t = _rope_tables(seq)
    tab = pl.BlockSpec((IN_ROWS, HEAD_DIM), lambda i: (i % tiles_per_seq, 0))
    n_planes = 3 * N_HEADS
    nat, cm, rg = pl.pallas_call(
        _inproj_kernel,
        grid=(rows // IN_ROWS,),
        in_specs=[
            pl.BlockSpec((IN_ROWS, d), lambda i: (i, 0)),
            pl.BlockSpec((1, d), lambda i: (0, 0)),
            pl.BlockSpec((d, IN_WIDTH), lambda i: (0, 0), pipeline_mode=pl.Buffered(1)),
            pl.BlockSpec((IN_ROWS, IN_ROWS), lambda i: (0, 0)),
            pl.BlockSpec((IN_ROWS, IN_ROWS), lambda i: (0, 0)),
            tab, tab, tab,
        ],
        out_specs=[
            pl.BlockSpec((2 * N_HEADS, IN_ROWS, HEAD_DIM), lambda i: (0, i, 0)),
            pl.BlockSpec((n_planes, None, MAX_DIL, half_rows, HEAD_DIM),
                         lambda i: (0, i // tiles_per_unit, 0, i % tiles_per_unit, 0)),
            pl.BlockSpec((IN_ROWS, 2 * LRU_WIDTH), lambda i: (i, 0)),
        ],
        out_shape=[
            jax.ShapeDtypeStruct((2 * N_HEADS, rows, HEAD_DIM), BF16),
            jax.ShapeDtypeStruct((n_planes, rows // UNIT, MAX_DIL, CLASS_ROWS, HEAD_DIM), BF16),
            jax.ShapeDtypeStruct((rows, 2 * LRU_WIDTH), F32),
        ],
        scratch_shapes=[pltpu.VMEM((IN_ROWS, d), BF16), pltpu.VMEM((IN_ROWS, d), BF16),
                        pltpu.VMEM((IN_ROWS, 3 * ATTN_WIDTH), BF16)],
        compiler_params=_params(("parallel",)),
        name="inproj",
    )(x, g_mix, w_in, _interleave_matrix(), _class_major_matrix(), cos_t, sa_t, sb_t)
    return nat, cm.reshape(n_planes, rows, HEAD_DIM), rg


def _attn_kernel(knp_ref, knm_ref, knn_ref, vnp_ref, vnm_ref, vnn_ref,
                 qc_ref, kcp_ref, kcm_ref, kcn_ref, vcp_ref, vcm_ref, vcn_ref,
                 out_ref, n1_ref, m1_ref, l1_ref, num_ref, max_ref, den_ref, *, units_per_seq):
    u = pl.program_id(1) % units_per_seq
    has_prev = u > 0
    has_next = u < units_per_seq - 1
    side = KEYS_PER_SIDE
    cat = lambda pieces: jnp.concatenate(pieces, axis=0)
    tiles = []

    n_tiles = UNIT // TQ1
    nk1 = TQ1 + 2 * side
    per_class = TQ1 // MAX_DIL
    row = lax.broadcasted_iota(jnp.int32, (TQ1, nk1), 0)
    col = lax.broadcasted_iota(jnp.int32, (TQ1, nk1), 1)
    q_pos = MAX_DIL * (row & (per_class - 1)) + (row >> 4)
    band = jnp.abs(col - side - q_pos) <= side

    def tile1(c):
        lo, hi = c * TQ1 - side, c * TQ1 + TQ1 + side
        pieces = [slice(r * CLASS_ROWS + c * per_class, r * CLASS_ROWS + (c + 1) * per_class) for r in range(MAX_DIL)]
        if c == 0:
            kv = lambda p, m, n: cat([p[...], m[0:hi, :]])
            mask = band & ((col >= side) | has_prev)
        elif c == n_tiles - 1:
            kv = lambda p, m, n: cat([m[lo:UNIT, :], n[...]])
            mask = band & ((col < TQ1 + side) | has_next)
        else:
            kv = lambda p, m, n: m[lo:hi, :]
            mask = band

        def finish(num, m, l):
            m = jnp.broadcast_to(m, (TQ1, HEAD_DIM))
            l = jnp.broadcast_to(l, (TQ1, HEAD_DIM))
            for r, rows_r in enumerate(pieces):
                got = slice(r * per_class, (r + 1) * per_class)
                n1_ref[rows_r, :] = num[got, :]
                m1_ref[rows_r, :] = m[got, :]
                l1_ref[rows_r, :] = l[got, :]

        return (lambda: cat([qc_ref[rows_r, :] for rows_r in pieces]), lambda: kv(knp_ref, knm_ref, knn_ref),
                lambda: kv(vnp_ref, vnm_ref, vnn_ref), mask, finish)

    tiles += [tile1(c) for c in range(n_tiles)]

    nk16 = 3 * CLASS_ROWS
    row = lax.broadcasted_iota(jnp.int32, (CLASS_ROWS, nk16), 0)
    col = lax.broadcasted_iota(jnp.int32, (CLASS_ROWS, nk16), 1)
    mask16 = ((jnp.abs(col - CLASS_ROWS - row) <= side) & ((col >= CLASS_ROWS) | has_prev)
              & ((col < 2 * CLASS_ROWS) | has_next))

    def tile16(r):
        sl = slice(r * CLASS_ROWS, (r + 1) * CLASS_ROWS)

        def finish(num, m, l):
            m1 = m1_ref[sl, :]
            m_new = jnp.maximum(m1, m)
            e1 = jnp.exp(m1 - m_new)
            e = jnp.exp(m - m_new)
            num_ref[sl, :] = e1 * n1_ref[sl, :] + e * num
            den_ref[sl, :] = e1 * l1_ref[sl, :] + e * l
            max_ref[sl, :] = m_new

        return (lambda: qc_ref[sl, :], lambda: cat([kcp_ref[sl, :], kcm_ref[sl, :], kcn_ref[sl, :]]),
                lambda: cat([vcp_ref[sl, :], vcm_ref[sl, :], vcn_ref[sl, :]]), mask16, finish)

    tiles += [tile16(r) for r in range(MAX_DIL)]

    edge = side // 4
    nq4 = 4 * CLASS_ROWS
    nk4 = nq4 + 2 * side
    row = lax.broadcasted_iota(jnp.int32, (nq4, nk4), 0)
    col = lax.broadcasted_iota(jnp.int32, (nq4, nk4), 1)
    q_idx = 4 * (row & (CLASS_ROWS - 1)) + (row >> 6)
    c_mid = col - side
    c_next = col - side - nq4
    k_idx = jnp.where(
        col < side, 4 * (CLASS_ROWS - edge + (col & (edge - 1))) + (col >> 4) - nq4,
        jnp.where(col < side + nq4, 4 * (c_mid & (CLASS_ROWS - 1)) + (c_mid >> 6),
                  4 * (c_next & (edge - 1)) + (c_next >> 4) + nq4))
    mask4 = ((jnp.abs(k_idx - q_idx) <= side) & ((col >= side) | has_prev)
             & ((col < side + nq4) | has_next))
    def tile4(r4):
        slabs = [slice((4 * s + r4) * CLASS_ROWS, (4 * s + r4 + 1) * CLASS_ROWS) for s in range(4)]

        def gather(ref, part):
            lo, hi = {"tail": (CLASS_ROWS - edge, CLASS_ROWS), "all": (0, CLASS_ROWS), "head": (0, edge)}[part]
            return [ref[sl.start + lo:sl.start + hi, :] for sl in slabs]

        def kv(p, m, n):
            return cat(gather(p, "tail") + gather(m, "all") + gather(n, "head"))

        def finish(num, m, l):
            m_old = cat([max_ref[sl, :] for sl in slabs])
            m_new = jnp.maximum(m_old, m)
            e_old = jnp.exp(m_old - m_new)
            e = jnp.exp(m - m_new)
            res = ((e_old * cat([num_ref[sl, :] for sl in slabs]) + e * num)
                   / (e_old * cat([den_ref[sl, :] for sl in slabs]) + e * l))
            for s, sl in enumerate(slabs):
                out_ref[sl, :] = res[CLASS_ROWS * s:CLASS_ROWS * (s + 1), :]

        return (lambda: cat(gather(qc_ref, "all")), lambda: kv(kcp_ref, kcm_ref, kcn_ref),
                lambda: kv(vcp_ref, vcm_ref, vcn_ref), mask4, finish)

    tiles += [tile4(r4) for r4 in range(4)]

    def scores(tile):
        return lax.dot_general(tile[0](), tile[1](), (((1,), (1,)), ((), ())), preferred_element_type=F32)

    pending = [scores(t) for t in tiles[:ATTN_SKEW]]
    for idx, (_, _, v, mask, finish) in enumerate(tiles):
        s = pending.pop(0)
        if idx + ATTN_SKEW < len(tiles):
            pending.append(scores(tiles[idx + ATTN_SKEW]))
        s = jnp.where(mask, s, NEG_INF)
        m = jnp.max(s, axis=-1, keepdims=True)
        p = jnp.exp(s - m)
        l = jnp.sum(p, axis=-1, keepdims=True)
        finish(jnp.dot(p.astype(BF16), v(), preferred_element_type=F32), m, l)


def _attention(nat, cm, seq):
    rows = nat.shape[1]
    units_per_seq = seq // UNIT
    ratio = UNIT // KEYS_PER_SIDE
    n_side_blocks = rows // KEYS_PER_SIDE
    n_units = rows // UNIT

    def unit(plane, shift):
        return pl.BlockSpec((None, UNIT, HEAD_DIM),
                            lambda h, u: (plane * N_HEADS + h, jnp.clip(u + shift, 0, n_units - 1), 0))

    def halo(plane, after):
        if after:
            index = lambda h, u: (plane * N_HEADS + h, jnp.minimum((u + 1) * ratio, n_side_blocks - 1), 0)
        else:
            index = lambda h, u: (plane * N_HEADS + h, jnp.maximum(u * ratio - 1, 0), 0)
        return pl.BlockSpec((None, KEYS_PER_SIDE, HEAD_DIM), index)

    nat_specs = [halo(0, False), unit(0, 0), halo(0, True), halo(1, False), unit(1, 0), halo(1, True)]
    cm_specs = [unit(0, 0), unit(1, -1), unit(1, 0), unit(1, 1), unit(2, -1), unit(2, 0), unit(2, 1)]
    return pl.pallas_call(
        functools.partial(_attn_kernel, units_per_seq=units_per_seq),
        grid=(N_HEADS, n_units),
        in_specs=nat_specs + cm_specs,
        out_specs=pl.BlockSpec((UNIT, HEAD_DIM), lambda h, u: (u, h)),
        out_shape=jax.ShapeDtypeStruct((rows, ATTN_WIDTH), F32),
        scratch_shapes=[pltpu.VMEM((UNIT, HEAD_DIM), F32)] * 6,
        compiler_params=_params(("parallel", "parallel")),
        name="attention",
    )(*([nat] * 6), *([cm] * 7))


def _lru_tile(xp_ref, xm_ref, xn_ref, cw_ref, cb_ref, wax_ref, bax_ref, lamc_ref,
              a_ref, b_ref, carry_ref, xs_ref, h_ref, *, t, tiles_per_seq, reverse, between=None):
    tt = IN_ROWS
    n_groups = tt // SUBLANES
    width = a_ref.shape[1]
    rowi = lax.broadcasted_iota(jnp.int32, (SUBLANES, width), 0)
    first = t == 0
    last = t == tiles_per_seq - 1

    def spread(row):
        return jnp.broadcast_to(row, (SUBLANES, width))

    before = jnp.where(first, 0.0, xp_ref[...])
    after = jnp.where(last, 0.0, xn_ref[...])
    for k in (2, 1):
        own = pltpu.roll(xm_ref[tt - k * SUBLANES:tt - (k - 1) * SUBLANES, :], 1, 0)
        prev_last = before[(3 - k) * SUBLANES - 1:(3 - k) * SUBLANES, :]
        xs_ref[(2 - k) * SUBLANES:(3 - k) * SUBLANES, :] = jnp.where(rowi == 0, spread(prev_last), own)
    xs_ref[2 * SUBLANES:2 * SUBLANES + tt, :] = xm_ref[...]
    xs_ref[2 * SUBLANES + tt:3 * SUBLANES + tt, :] = jnp.where(
        rowi == SUBLANES - 1, spread(after[0:1, :]), pltpu.roll(xm_ref[0:SUBLANES, :], SUBLANES - 1, 0))
    u = cb_ref[...]
    for j in range(4):
        u = u + xs_ref[j * SUBLANES:j * SUBLANES + tt, :] * cw_ref[j:j + 1, :]
    ub = u.astype(BF16)

    pres = [jnp.dot(ub[:, g * LRU_BLOCK:(g + 1) * LRU_BLOCK], wax_ref[g], preferred_element_type=F32) + bax_ref[g]
            for g in range(N_LRU_BLOCKS)]
    if between is not None:
        between()

    for g in range(N_LRU_BLOCKS):
        sl = slice(g * LRU_BLOCK, (g + 1) * LRU_BLOCK)
        gates = 0.5 + 0.5 * jnp.tanh(0.5 * pres[g])
        r = gates[:, :LRU_BLOCK]
        ig = gates[:, LRU_BLOCK:]
        log_a = lamc_ref[:, sl] * r
        a = jnp.exp(log_a)
        a_ref[:, sl] = a
        b_ref[:, sl] = jnp.sqrt(jnp.tanh(-log_a) * (1.0 + a * a)) * (ig * u[:, sl])

    def group(gi):
        g = (n_groups - 1 - gi) if reverse else gi
        return pl.ds(pl.multiple_of(g * SUBLANES, SUBLANES), SUBLANES)

    def scan(gi, state):
        h, p = state
        rows_g = group(gi)
        a = a_ref[rows_g, :]
        h = a * h + b_ref[rows_g, :]
        p = a * p
        a_ref[rows_g, :] = p
        b_ref[rows_g, :] = h
        return h, p

    zeros = jnp.zeros((SUBLANES, width), F32)
    h_end, p_end = lax.fori_loop(0, n_groups, scan, (zeros, zeros + 1.0), unroll=4)

    starts_seq = last if reverse else first
    incoming = spread(jnp.where(starts_seq, 0.0, carry_ref[0:1, :]))
    head = (SUBLANES - 1) if reverse else 0
    enter = incoming
    for _ in range(SUBLANES):
        leave = h_end + p_end * enter
        enter = jnp.where(rowi == head, incoming, pltpu.roll(leave, (SUBLANES - 1) if reverse else 1, 0))
    tail = 0 if reverse else SUBLANES - 1
    carry_ref[...] = spread((h_end + p_end * enter)[tail:tail + 1, :])

    def fix(gi, _):
        rows_g = group(gi)
        h_ref[rows_g, :] = b_ref[rows_g, :] + a_ref[rows_g, :] * enter
        return 0

    lax.fori_loop(0, n_groups, fix, 0, unroll=4)


def _lru_rev_kernel(xp_ref, xm_ref, xn_ref, cw_ref, cb_ref, wax_ref, bax_ref, lamc_ref,
                    out_ref, a_ref, b_ref, carry_ref, xs_ref, *, n_tiles, tiles_per_seq):
    t = (n_tiles - 1 - pl.program_id(0)) % tiles_per_seq
    _lru_tile(xp_ref, xm_ref, xn_ref, cw_ref, cb_ref, wax_ref, bax_ref, lamc_ref,
              a_ref, b_ref, carry_ref, xs_ref, out_ref, t=t, tiles_per_seq=tiles_per_seq, reverse=True)


def _lru_out_kernel(xp_ref, xm_ref, xn_ref, gate_ref, other_ref, x_ref, attn_ref,
                    cw_ref, cb_ref, wax_ref, bax_ref, lamc_ref, ga_ref, gl_ref, w_ref, unperm_ref, unclass_ref,
                    out_ref, a_ref, b_ref, carry_ref, xs_ref, h_ref, mixed_ref, stage_ref, *, n_tiles, tiles_per_seq):
    i = pl.program_id(0)

    @pl.when(i == 0)
    def _():
        mixed_ref[...] = jnp.zeros_like(mixed_ref)

    half_rows = IN_ROWS // MAX_DIL
    for r in range(MAX_DIL):
        stage_ref[r * half_rows:(r + 1) * half_rows, :] = _rms(attn_ref[r], ga_ref[...]).astype(BF16)
    attn_mixed = jnp.dot(unclass_ref[...], stage_ref[...], preferred_element_type=F32).astype(BF16)
    mixed = jnp.concatenate([attn_mixed, mixed_ref[...]], axis=1)
    half = out_ref.shape[1] // 2

    def project(cols):
        out_ref[:, cols] = x_ref[:, cols] + jnp.dot(mixed, w_ref[:, cols], preferred_element_type=F32)

    project(slice(0, half))
    t = jnp.minimum(i, n_tiles - 1) % tiles_per_seq
    _lru_tile(xp_ref, xm_ref, xn_ref, cw_ref, cb_ref, wax_ref, bax_ref, lamc_ref,
              a_ref, b_ref, carry_ref, xs_ref, h_ref, t=t, tiles_per_seq=tiles_per_seq, reverse=False,
              between=lambda: project(slice(half, 2 * half)))
    lru = (h_ref[...] + other_ref[...]) * _gelu_tanh(gate_ref[...])
    mixed_ref[...] = jnp.dot(unperm_ref[...], _rms(lru, gl_ref[...]).astype(BF16),
                             preferred_element_type=F32).astype(BF16)


def _lru_specs(rows, tile):
    tt = IN_ROWS
    before_rows = 2 * SUBLANES
    n_after_blocks = rows // SUBLANES
    x_specs = [
        pl.BlockSpec((before_rows, LRU_WIDTH), lambda i: (jnp.maximum(tile(i) * (tt // before_rows) - 1, 0), 0)),
        pl.BlockSpec((tt, LRU_WIDTH), lambda i: (tile(i), 0)),
        pl.BlockSpec((SUBLANES, LRU_WIDTH),
                     lambda i: (jnp.minimum((tile(i) + 1) * (tt // SUBLANES), n_after_blocks - 1), 0)),
    ]
    w_specs = [
        pl.BlockSpec((4, LRU_WIDTH), lambda i: (0, 0)),
        pl.BlockSpec((1, LRU_WIDTH), lambda i: (0, 0)),
        pl.BlockSpec((N_LRU_BLOCKS, LRU_BLOCK, 2 * LRU_BLOCK), lambda i: (0, 0, 0)),
        pl.BlockSpec((N_LRU_BLOCKS, 1, 2 * LRU_BLOCK), lambda i: (0, 0, 0)),
        pl.BlockSpec((1, LRU_WIDTH), lambda i: (0, 0)),
    ]
    scratch = [pltpu.VMEM((tt, LRU_WIDTH), F32), pltpu.VMEM((tt, LRU_WIDTH), F32),
               pltpu.VMEM((SUBLANES, LRU_WIDTH), F32), pltpu.VMEM((tt + 3 * SUBLANES, LRU_WIDTH), F32)]
    return x_specs, w_specs, scratch


def _lru_reverse(rg, conv_w, conv_b, wax, bax, lamc, seq):
    rows = rg.shape[0]
    tt = IN_ROWS
    n_tiles = rows // tt
    tile = lambda i: n_tiles - 1 - i
    x_specs, w_specs, scratch = _lru_specs(rows, tile)
    return pl.pallas_call(
        functools.partial(_lru_rev_kernel, n_tiles=n_tiles, tiles_per_seq=seq // tt),
        grid=(n_tiles,),
        in_specs=x_specs + w_specs,
        out_specs=pl.BlockSpec((tt, LRU_WIDTH), lambda i: (tile(i), 0)),
        out_shape=jax.ShapeDtypeStruct((rows, LRU_WIDTH), F32),
        scratch_shapes=scratch,
        compiler_params=_params(("arbitrary",)),
        name="lru_rev",
    )(rg, rg, rg, conv_w, conv_b, wax, bax, lamc)


def _lru_forward_outproj(x, rg, h_rev, attn, conv_w, conv_b, wax, bax, lamc, g_attn, g_lru, w_out, seq):
    rows, d = x.shape
    tt = IN_ROWS
    n_tiles = rows // tt
    tile = lambda i: jnp.minimum(i, n_tiles - 1)
    prev = lambda i: jnp.maximum(i - 1, 0)
    tiles_per_unit = UNIT // tt
    x_specs, w_specs, scratch = _lru_specs(rows, tile)
    x_specs += [
        pl.BlockSpec((tt, LRU_WIDTH), lambda i: (tile(i), 1)),
        pl.BlockSpec((tt, LRU_WIDTH), lambda i: (tile(i), 0)),
        pl.BlockSpec((tt, d), lambda i: (prev(i), 0)),
        pl.BlockSpec((None, MAX_DIL, tt // MAX_DIL, ATTN_WIDTH),
                     lambda i: (prev(i) // tiles_per_unit, 0, prev(i) % tiles_per_unit, 0)),
    ]
    w_specs += [
        pl.BlockSpec((1, ATTN_WIDTH), lambda i: (0, 0)),
        pl.BlockSpec((1, LRU_WIDTH), lambda i: (0, 0)),
        pl.BlockSpec(w_out.shape, lambda i: (0, 0), pipeline_mode=pl.Buffered(1)),
        pl.BlockSpec((tt, tt), lambda i: (0, 0)),
        pl.BlockSpec((tt, tt), lambda i: (0, 0)),
    ]
    scratch += [pltpu.VMEM((tt, LRU_WIDTH), F32), pltpu.VMEM((tt, LRU_WIDTH), BF16), pltpu.VMEM((tt, ATTN_WIDTH), BF16)]
    return pl.pallas_call(
        functools.partial(_lru_out_kernel, n_tiles=n_tiles, tiles_per_seq=seq // tt),
        grid=(n_tiles + 1,),
        in_specs=x_specs + w_specs,
        out_specs=pl.BlockSpec((tt, d), lambda i: (prev(i), 0)),
        out_shape=jax.ShapeDtypeStruct((rows, d), F32),
        scratch_shapes=scratch,
        compiler_params=_params(("arbitrary",)),
        name="lru_fwd_outproj",
    )(rg, rg, rg, rg, h_rev, x, attn.reshape(rows // UNIT, MAX_DIL, CLASS_ROWS, ATTN_WIDTH), conv_w, conv_b, wax, bax, lamc,
      g_attn, g_lru, w_out, _interleave_matrix().T, _class_major_matrix().T)


def _mlp_kernel(xp_ref, xm_ref, xn_ref, g_ref, wg_ref, wv_ref, cwg_ref, cwv_ref, cbg_ref, cbv_ref,
                wd_ref, gf_ref, out_ref, h_ref, *, tm, tiles_per_seq):
    i = pl.program_id(0)
    k = pl.program_id(1)
    t = i % tiles_per_seq
    n = tm + BF16_ROWS

    @pl.when(k == 0)
    def _():
        g = g_ref[...]
        h_ref[0:tm, :] = _rms(xm_ref[...], g).astype(BF16)
        after = jnp.where(t == tiles_per_seq - 1, 0.0, _rms(xn_ref[...], g))
        before = jnp.where(t == 0, 0.0, _rms(xp_ref[...], g))
        h_ref[tm:n, :] = jnp.concatenate([after, before], axis=0).astype(BF16)
        out_ref[...] = xm_ref[...]

    h = h_ref[...]

    def conv(w_ref, cw_ref, cb_ref):
        u = jnp.dot(h, w_ref[...], preferred_element_type=F32)
        c = (cb_ref[...] + pltpu.roll(u, 1, 0) * cw_ref[0:1, :] + u * cw_ref[1:2, :]
             + pltpu.roll(u, n - 1, 0) * cw_ref[2:3, :])
        return c[0:tm, :]

    act = (_gelu_tanh(conv(wg_ref, cwg_ref, cbg_ref)) * conv(wv_ref, cwv_ref, cbv_ref)).astype(BF16)
    out_ref[...] += jnp.dot(act, wd_ref[...], preferred_element_type=F32)

    @pl.when(k == pl.num_programs(1) - 1)
    def _():
        out_ref[...] = _rms(out_ref[...], gf_ref[...])


def _mlp(x, g_mlp, w_up, conv_w, conv_b, w_down, g_final, seq, tm, tf):
    rows, d = x.shape
    d_ff = w_down.shape[0]
    tm = min(tm, seq)
    tf = min(tf, d_ff)
    nk = d_ff // tf
    tiles_per_seq = seq // tm
    ratio = tm // SUBLANES
    n_halo_blocks = rows // SUBLANES
    return pl.pallas_call(
        functools.partial(_mlp_kernel, tm=tm, tiles_per_seq=tiles_per_seq),
        grid=(rows // tm, nk),
        in_specs=[
            pl.BlockSpec((SUBLANES, d), lambda i, k: (jnp.maximum(i * ratio - 1, 0), 0)),
            pl.BlockSpec((tm, d), lambda i, k: (i, 0)),
            pl.BlockSpec((SUBLANES, d), lambda i, k: (jnp.minimum((i + 1) * ratio, n_halo_blocks - 1), 0)),
            pl.BlockSpec((1, d), lambda i, k: (0, 0)),
            pl.BlockSpec((d, tf), lambda i, k: (0, k)),
            pl.BlockSpec((d, tf), lambda i, k: (0, nk + k)),
            pl.BlockSpec((3, tf), lambda i, k: (0, k)),
            pl.BlockSpec((3, tf), lambda i, k: (0, nk + k)),
            pl.BlockSpec((1, tf), lambda i, k: (0, k)),
            pl.BlockSpec((1, tf), lambda i, k: (0, nk + k)),
            pl.BlockSpec((tf, d), lambda i, k: (k, 0)),
            pl.BlockSpec((1, d), lambda i, k: (0, 0)),
        ],
        out_specs=pl.BlockSpec((tm, d), lambda i, k: (i, 0)),
        out_shape=jax.ShapeDtypeStruct((rows, d), F32),
        scratch_shapes=[pltpu.VMEM((tm + BF16_ROWS, d), BF16)],
        compiler_params=_params(("parallel", "arbitrary")),
        name="mlp",
    )(x, x, x, g_mlp, w_up, w_up, conv_w, conv_w, conv_b, conv_b, w_down, g_final)


def _prepare(g_mix, w_in, w_out, g_attn_out, g_lru_out, conv_rg_w, conv_rg_b, rg_w_a, rg_b_a,
             rg_w_x, rg_b_x, rg_lam, g_mlp, w_up, conv_ff_w, conv_ff_b, w_down, g_final):
    l = 0
    row = lambda v: v.reshape(1, -1).astype(F32)
    wax = jnp.concatenate([rg_w_a[l], rg_w_x[l]], axis=-1).astype(BF16)
    bax = jnp.concatenate([rg_b_a[l].reshape(2, N_LRU_BLOCKS, 1, LRU_BLOCK),
                           rg_b_x[l].reshape(2, N_LRU_BLOCKS, 1, LRU_BLOCK)], axis=-1).astype(F32)
    lamc = (-LRU_C * jax.nn.softplus(-rg_lam[l].astype(F32))).reshape(2, 1, LRU_WIDTH)
    return dict(
        g_mix=row(g_mix[l]), w_in=w_in[l].astype(BF16), w_out=w_out[l].astype(BF16),
        g_attn=row(g_attn_out[l]), g_lru=row(g_lru_out[l]),
        conv_rg_w=conv_rg_w[l].astype(F32), conv_rg_b=row(conv_rg_b[l]),
        wax=wax, bax=bax, lamc=lamc,
        g_mlp=row(g_mlp[l]), w_up=w_up[l].astype(BF16), conv_ff_w=conv_ff_w[l].astype(F32),
        conv_ff_b=row(conv_ff_b[l]), w_down=w_down[l].astype(BF16), g_final=row(g_final),
    )


def _encode(x, p, *, tm_mlp=512, tf=1024):
    batch, seq, d = x.shape
    x2 = x.reshape(batch * seq, d)
    nat, cm, rg = _inproj(x2, p["g_mix"], p["w_in"], seq)
    attn = _attention(nat, cm, seq)
    h_rev = _lru_reverse(rg, p["conv_rg_w"], p["conv_rg_b"], p["wax"][1], p["bax"][1], p["lamc"][1], seq)
    x1 = _lru_forward_outproj(x2, rg, h_rev, attn, p["conv_rg_w"], p["conv_rg_b"], p["wax"][0], p["bax"][0],
                              p["lamc"][0], p["g_attn"], p["g_lru"], p["w_out"], seq)
    y = _mlp(x1, p["g_mlp"], p["w_up"], p["conv_ff_w"], p["conv_ff_b"], p["w_down"], p["g_final"],
             seq, tm_mlp, tf)
    return y.reshape(batch, seq, d)


def kernel(x_prompt, x_sample, g_mix, w_in, w_out, g_attn_out, g_lru_out, conv_rg_w, conv_rg_b,
           rg_w_a, rg_b_a, rg_w_x, rg_b_x, rg_lam, g_mlp, w_up, conv_ff_w, conv_ff_b, w_down, g_final):
    p = _prepare(g_mix, w_in, w_out, g_attn_out, g_lru_out, conv_rg_w, conv_rg_b, rg_w_a, rg_b_a,
                 rg_w_x, rg_b_x, rg_lam, g_mlp, w_up, conv_ff_w, conv_ff_b, w_down, g_final)
    return (_encode(x_prompt, p), _encode(x_sample, p))
```

```python
import functools
import math

import jax
import jax.numpy as jnp
from jax import lax
from jax.experimental import pallas as pl
from jax.experimental.pallas import tpu as pltpu

F32 = jnp.float32
BF16 = jnp.bfloat16

HEAD_DIM = 128
N_HEADS = 8
ATTN_WIDTH = N_HEADS * HEAD_DIM
LRU_WIDTH = 1024
N_LRU_BLOCKS = 8
LRU_BLOCK = LRU_WIDTH // N_LRU_BLOCKS
IN_WIDTH = 3 * ATTN_WIDTH + 2 * LRU_WIDTH
MAX_DIL = 16
KEYS_PER_SIDE = 64
UNIT = MAX_DIL * KEYS_PER_SIDE
CLASS_ROWS = UNIT // MAX_DIL
TQ1 = 256
ATTN_SKEW = 4
ATTN_HEADS = 4
IN_ROWS = 512
IN_CHUNK = 256
IN_GROUP = 512
ROPE_THETA = 500000.0
ROPE_HALF = HEAD_DIM // 8
LRU_C = 8.0
NORM_EPS = 1e-6
NEG_INF = -1e30

SUBLANES = 8
BF16_ROWS = 16
VMEM_LIMIT = 60 * 1024 * 1024


def _params(semantics):
    return pltpu.CompilerParams(dimension_semantics=semantics, vmem_limit_bytes=VMEM_LIMIT)


def _rms(x, g):
    return x * lax.rsqrt(jnp.mean(x * x, axis=-1, keepdims=True) + NORM_EPS) * g


def _gelu_tanh(x):
    return x * (0.5 * (1.0 + jnp.tanh(math.sqrt(2.0 / math.pi) * (x + 0.044715 * (x * x * x)))))


def _inproj_kernel(x_ref, g_ref, w_ref, perm_ref, class_ref, cos_ref, sa_ref, sb_ref, nat_ref, cm_ref, rg_ref,
                   h_ref, hil_ref, zb_ref):
    h_ref[...] = _rms(x_ref[...], g_ref[...]).astype(BF16)
    hil_ref[...] = jnp.dot(perm_ref[...], h_ref[...], preferred_element_type=F32).astype(BF16)

    heads_per_chunk = IN_CHUNK // HEAD_DIM
    n_chunks = IN_WIDTH // IN_CHUNK
    n_qkv_chunks = 3 * ATTN_WIDTH // IN_CHUNK

    def project(c):
        cols = slice(c * IN_CHUNK, (c + 1) * IN_CHUNK)
        lhs = h_ref if c < n_qkv_chunks else hil_ref
        return jnp.dot(lhs[...], w_ref[:, cols], preferred_element_type=F32)

    half_rows = IN_ROWS // MAX_DIL
    scale = HEAD_DIM ** -0.5
    rotary = {0: (cos_ref[...] * scale, sa_ref[...] * scale, sb_ref[...] * scale),
              1: (cos_ref[...], sa_ref[...], sb_ref[...])}

    z_next = project(0)
    for c in range(n_chunks):
        z = z_next
        if c + 1 < n_chunks:
            z_next = project(c + 1)
        if c >= n_qkv_chunks:
            off = (c - n_qkv_chunks) * IN_CHUNK
            rg_ref[:, off:off + IN_CHUNK] = z
            continue
        for hc in range(heads_per_chunk):
            plane = c * heads_per_chunk + hc
            t = z[:, hc * HEAD_DIM:(hc + 1) * HEAD_DIM]
            if plane // N_HEADS in rotary:
                cos, sa, sb = rotary[plane // N_HEADS]
                t = t * cos + pltpu.roll(t, HEAD_DIM - ROPE_HALF, 1) * sa + pltpu.roll(t, ROPE_HALF, 1) * sb
            t = t.astype(BF16)
            zb_ref[:, plane * HEAD_DIM:(plane + 1) * HEAD_DIM] = t
            if plane >= N_HEADS:
                nat_ref[plane - N_HEADS] = t
        done = (c + 1) * IN_CHUNK
        if done % IN_GROUP == 0:
            cols = slice(done - IN_GROUP, done)
            grouped = jnp.dot(class_ref[...], zb_ref[:, cols], preferred_element_type=F32).astype(BF16)
            for hg in range(IN_GROUP // HEAD_DIM):
                plane = (done - IN_GROUP) // HEAD_DIM + hg
                for r in range(MAX_DIL):
                    cm_ref[plane, r] = grouped[r * half_rows:(r + 1) * half_rows, hg * HEAD_DIM:(hg + 1) * HEAD_DIM]


def _class_major_matrix():
    per_class = IN_ROWS // MAX_DIL
    row = jnp.arange(IN_ROWS)
    position = MAX_DIL * (row % per_class) + row // per_class
    return (position[:, None] == jnp.arange(IN_ROWS)[None, :]).astype(BF16)


def _interleave_matrix():
    n_groups = IN_ROWS // SUBLANES
    row = jnp.arange(IN_ROWS)
    time = n_groups * (row % SUBLANES) + row // SUBLANES
    return (time[:, None] == jnp.arange(IN_ROWS)[None, :]).astype(BF16)


def _rope_tables(seq):
    inv = ROPE_THETA ** (-jnp.arange(ROPE_HALF, dtype=F32) / ROPE_HALF)
    ang = jnp.arange(seq).astype(F32)[:, None] * inv[None, :]
    cos, sin = jnp.cos(ang), jnp.sin(ang)
    rest = HEAD_DIM - 2 * ROPE_HALF
    cos_t = jnp.concatenate([cos, cos, jnp.ones((seq, rest), F32)], axis=1)
    sa_t = jnp.concatenate([-sin, jnp.zeros((seq, HEAD_DIM - ROPE_HALF), F32)], axis=1)
    sb_t = jnp.concatenate([jnp.zeros((seq, ROPE_HALF), F32), sin, jnp.zeros((seq, rest), F32)], axis=1)
    return cos_t, sa_t, sb_t


def _inproj(x, g_mix, w_in, seq):
    rows, d = x.shape
    tiles_per_seq = seq // IN_ROWS
    tiles_per_unit = UNIT // IN_ROWS
    half_rows = IN_ROWS // MAX_DIL
    cos_t, sa_t, sb_t = _rope_tables(seq)
    tab = pl.BlockSpec((IN_ROWS, HEAD_DIM), lambda i: (i % tiles_per_seq, 0))
    n_planes = 3 * N_HEADS
    nat, cm, rg = pl.pallas_call(
        _inproj_kernel,
        grid=(rows // IN_ROWS,),
        in_specs=[
            pl.BlockSpec((IN_ROWS, d), lambda i: (i, 0)),
            pl.BlockSpec((1, d), lambda i: (0, 0)),
            pl.BlockSpec((d, IN_WIDTH), lambda i: (0, 0), pipeline_mode=pl.Buffered(1)),
            pl.BlockSpec((IN_ROWS, IN_ROWS), lambda i: (0, 0)),
            pl.BlockSpec((IN_ROWS, IN_ROWS), lambda i: (0, 0)),
            tab, tab, tab,
        ],
        out_specs=[
            pl.BlockSpec((2 * N_HEADS, IN_ROWS, HEAD_DIM), lambda i: (0, i, 0)),
            pl.BlockSpec((n_planes, None, MAX_DIL, half_rows, HEAD_DIM),
                         lambda i: (0, i // tiles_per_unit, 0, i % tiles_per_unit, 0)),
            pl.BlockSpec((IN_ROWS, 2 * LRU_WIDTH), lambda i: (i, 0)),
        ],
        out_shape=[
            jax.ShapeDtypeStruct((2 * N_HEADS, rows, HEAD_DIM), BF16),
            jax.ShapeDtypeStruct((n_planes, rows // UNIT, MAX_DIL, CLASS_ROWS, HEAD_DIM), BF16),
            jax.ShapeDtypeStruct((rows, 2 * LRU_WIDTH), F32),
        ],
        scratch_shapes=[pltpu.VMEM((IN_ROWS, d), BF16), pltpu.VMEM((IN_ROWS, d), BF16),
                        pltpu.VMEM((IN_ROWS, 3 * ATTN_WIDTH), BF16)],
        compiler_params=_params(("parallel",)),
        name="inproj",
    )(x, g_mix, w_in, _interleave_matrix(), _class_major_matrix(), cos_t, sa_t, sb_t)
    return nat, cm.reshape(n_planes, rows, HEAD_DIM), rg


def _attn_head_tiles(knp_ref, knm_ref, knn_ref, vnp_ref, vnm_ref, vnn_ref,
                     qc_ref, kcp_ref, kcm_ref, kcn_ref, vcp_ref, vcm_ref, vcn_ref,
                     out_ref, n1_ref, m1_ref, l1_ref, num_ref, max_ref, den_ref, *, has_prev, has_next):
    side = KEYS_PER_SIDE
    cat = lambda pieces: jnp.concatenate(pieces, axis=0)
    tiles = []

    n_tiles = UNIT // TQ1
    nk1 = TQ1 + 2 * side
    per_class = TQ1 // MAX_DIL
    row = lax.broadcasted_iota(jnp.int32, (TQ1, nk1), 0)
    col = lax.broadcasted_iota(jnp.int32, (TQ1, nk1), 1)
    q_pos = MAX_DIL * (row & (per_class - 1)) + (row >> 4)
    band = jnp.abs(col - side - q_pos) <= side

    def tile1(c):
        lo, hi = c * TQ1 - side, c * TQ1 + TQ1 + side
        pieces = [slice(r * CLASS_ROWS + c * per_class, r * CLASS_ROWS + (c + 1) * per_class) for r in range(MAX_DIL)]
        if c == 0:
            kv = lambda p, m, n: cat([p[...], m[0:hi, :]])
            mask = band & ((col >= side) | has_prev)
        elif c == n_tiles - 1:
            kv = lambda p, m, n: cat([m[lo:UNIT, :], n[...]])
            mask = band & ((col < TQ1 + side) | has_next)
        else:
            kv = lambda p, m, n: m[lo:hi, :]
            mask = band

        def finish(num, m, l):
            m = jnp.broadcast_to(m, (TQ1, HEAD_DIM))
            l = jnp.broadcast_to(l, (TQ1, HEAD_DIM))
            for r, rows_r in enumerate(pieces):
                got = slice(r * per_class, (r + 1) * per_class)
                n1_ref[rows_r, :] = num[got, :]
                m1_ref[rows_r, :] = m[got, :]
                l1_ref[rows_r, :] = l[got, :]

        return (lambda: cat([qc_ref[rows_r, :] for rows_r in pieces]), lambda: kv(knp_ref, knm_ref, knn_ref),
                lambda: kv(vnp_ref, vnm_ref, vnn_ref), mask, finish)

    tiles += [tile1(c) for c in range(n_tiles)]

    nk16 = 3 * CLASS_ROWS
    row = lax.broadcasted_iota(jnp.int32, (CLASS_ROWS, nk16), 0)
    col = lax.broadcasted_iota(jnp.int32, (CLASS_ROWS, nk16), 1)
    mask16 = ((jnp.abs(col - CLASS_ROWS - row) <= side) & ((col >= CLASS_ROWS) | has_prev)
              & ((col < 2 * CLASS_ROWS) | has_next))

    def tile16(r):
        sl = slice(r * CLASS_ROWS, (r + 1) * CLASS_ROWS)

        def finish(num, m, l):
            m1 = m1_ref[sl, :]
            m_new = jnp.maximum(m1, m)
            e1 = jnp.exp(m1 - m_new)
            e = jnp.exp(m - m_new)
            num_ref[sl, :] = e1 * n1_ref[sl, :] + e * num
            den_ref[sl, :] = e1 * l1_ref[sl, :] + e * l
            max_ref[sl, :] = m_new

        return (lambda: qc_ref[sl, :], lambda: cat([kcp_ref[sl, :], kcm_ref[sl, :], kcn_ref[sl, :]]),
                lambda: cat([vcp_ref[sl, :], vcm_ref[sl, :], vcn_ref[sl, :]]), mask16, finish)

    tiles += [tile16(r) for r in range(MAX_DIL)]

    edge = side // 4
    nq4 = 4 * CLASS_ROWS
    nk4 = nq4 + 2 * side
    row = lax.broadcasted_iota(jnp.int32, (nq4, nk4), 0)
    col = lax.broadcasted_iota(jnp.int32, (nq4, nk4), 1)
    q_idx = 4 * (row & (CLASS_ROWS - 1)) + (row >> 6)
    c_mid = col - side
    c_next = col - side - nq4
    k_idx = jnp.where(
        col < side, 4 * (CLASS_ROWS - edge + (col & (edge - 1))) + (col >> 4) - nq4,
        jnp.where(col < side + nq4, 4 * (c_mid & (CLASS_ROWS - 1)) + (c_mid >> 6),
                  4 * (c_next & (edge - 1)) + (c_next >> 4) + nq4))
    mask4 = ((jnp.abs(k_idx - q_idx) <= side) & ((col >= side) | has_prev)
             & ((col < side + nq4) | has_next))
    def tile4(r4):
        slabs = [slice((4 * s + r4) * CLASS_ROWS, (4 * s + r4 + 1) * CLASS_ROWS) for s in range(4)]

        def gather(ref, part):
            lo, hi = {"tail": (CLASS_ROWS - edge, CLASS_ROWS), "all": (0, CLASS_ROWS), "head": (0, edge)}[part]
            return [ref[sl.start + lo:sl.start + hi, :] for sl in slabs]

        def kv(p, m, n):
            return cat(gather(p, "tail") + gather(m, "all") + gather(n, "head"))

        def finish(num, m, l):
            m_old = cat([max_ref[sl, :] for sl in slabs])
            m_new = jnp.maximum(m_old, m)
            e_old = jnp.exp(m_old - m_new)
            e = jnp.exp(m - m_new)
            res = ((e_old * cat([num_ref[sl, :] for sl in slabs]) + e * num)
                   / (e_old * cat([den_ref[sl, :] for sl in slabs]) + e * l))
            for s, sl in enumerate(slabs):
                out_ref[sl, :] = res[CLASS_ROWS * s:CLASS_ROWS * (s + 1), :]

        return (lambda: cat(gather(qc_ref, "all")), lambda: kv(kcp_ref, kcm_ref, kcn_ref),
                lambda: kv(vcp_ref, vcm_ref, vcn_ref), mask4, finish)

    tiles += [tile4(r4) for r4 in range(4)]
    return tiles


def _attn_kernel(*refs, units_per_seq):
    n_in = 13
    ins, out_ref, scratch = refs[:n_in], refs[n_in], refs[n_in + 1:]
    u = pl.program_id(1) % units_per_seq
    tiles = []
    for hh in range(ATTN_HEADS):
        tiles += _attn_head_tiles(*[ref.at[hh] for ref in ins], out_ref.at[:, hh * HEAD_DIM:(hh + 1) * HEAD_DIM],
                                  *scratch[6 * hh:6 * (hh + 1)], has_prev=u > 0, has_next=u < units_per_seq - 1)

    def scores(tile):
        return lax.dot_general(tile[0](), tile[1](), (((1,), (1,)), ((), ())), preferred_element_type=F32)

    pending = [scores(t) for t in tiles[:ATTN_SKEW]]
    for idx, (_, _, v, mask, finish) in enumerate(tiles):
        s = pending.pop(0)
        if idx + ATTN_SKEW < len(tiles):
            pending.append(scores(tiles[idx + ATTN_SKEW]))
        s = jnp.where(mask, s, NEG_INF)
        m = jnp.max(s, axis=-1, keepdims=True)
        p = jnp.exp(s - m)
        l = jnp.sum(p, axis=-1, keepdims=True)
        finish(jnp.dot(p.astype(BF16), v(), preferred_element_type=F32), m, l)


def _attention(nat, cm, seq):
    rows = nat.shape[1]
    units_per_seq = seq // UNIT
    ratio = UNIT // KEYS_PER_SIDE
    n_side_blocks = rows // KEYS_PER_SIDE
    n_units = rows // UNIT

    blocks_per_plane = N_HEADS // ATTN_HEADS

    def unit(plane, shift):
        return pl.BlockSpec((ATTN_HEADS, UNIT, HEAD_DIM),
                            lambda h, u: (plane * blocks_per_plane + h, jnp.clip(u + shift, 0, n_units - 1), 0))

    def halo(plane, after):
        if after:
            index = lambda h, u: (plane * blocks_per_plane + h, jnp.minimum((u + 1) * ratio, n_side_blocks - 1), 0)
        else:
            index = lambda h, u: (plane * blocks_per_plane + h, jnp.maximum(u * ratio - 1, 0), 0)
        return pl.BlockSpec((ATTN_HEADS, KEYS_PER_SIDE, HEAD_DIM), index)

    nat_specs = [halo(0, False), unit(0, 0), halo(0, True), halo(1, False), unit(1, 0), halo(1, True)]
    cm_specs = [unit(0, 0), unit(1, -1), unit(1, 0), unit(1, 1), unit(2, -1), unit(2, 0), unit(2, 1)]
    return pl.pallas_call(
        functools.partial(_attn_kernel, units_per_seq=units_per_seq),
        grid=(blocks_per_plane, n_units),
        in_specs=nat_specs + cm_specs,
        out_specs=pl.BlockSpec((UNIT, ATTN_HEADS * HEAD_DIM), lambda h, u: (u, h)),
        out_shape=jax.ShapeDtypeStruct((rows, ATTN_WIDTH), F32),
        scratch_shapes=[pltpu.VMEM((UNIT, HEAD_DIM), F32)] * (6 * ATTN_HEADS),
        compiler_params=_params(("parallel", "parallel")),
        name="attention",
    )(*([nat] * 6), *([cm] * 7))


def _lru_tile(xp_ref, xm_ref, xn_ref, cw_ref, cb_ref, wax_ref, bax_ref, lamc_ref,
              a_ref, b_ref, carry_ref, xs_ref, h_ref, *, t, tiles_per_seq, reverse, between=None):
    tt = IN_ROWS
    n_groups = tt // SUBLANES
    width = a_ref.shape[1]
    rowi = lax.broadcasted_iota(jnp.int32, (SUBLANES, width), 0)
    first = t == 0
    last = t == tiles_per_seq - 1

    def spread(row):
        return jnp.broadcast_to(row, (SUBLANES, width))

    before = jnp.where(first, 0.0, xp_ref[...])
    after = jnp.where(last, 0.0, xn_ref[...])
    for k in (2, 1):
        own = pltpu.roll(xm_ref[tt - k * SUBLANES:tt - (k - 1) * SUBLANES, :], 1, 0)
        prev_last = before[(3 - k) * SUBLANES - 1:(3 - k) * SUBLANES, :]
        xs_ref[(2 - k) * SUBLANES:(3 - k) * SUBLANES, :] = jnp.where(rowi == 0, spread(prev_last), own)
    xs_ref[2 * SUBLANES:2 * SUBLANES + tt, :] = xm_ref[...]
    xs_ref[2 * SUBLANES + tt:3 * SUBLANES + tt, :] = jnp.where(
        rowi == SUBLANES - 1, spread(after[0:1, :]), pltpu.roll(xm_ref[0:SUBLANES, :], SUBLANES - 1, 0))
    u = cb_ref[...]
    for j in range(4):
        u = u + xs_ref[j * SUBLANES:j * SUBLANES + tt, :] * cw_ref[j:j + 1, :]
    ub = u.astype(BF16)

    pres = [jnp.dot(ub[:, g * LRU_BLOCK:(g + 1) * LRU_BLOCK], wax_ref[g], preferred_element_type=F32) + bax_ref[g]
            for g in range(N_LRU_BLOCKS)]
    if between is not None:
        between()

    for g in range(N_LRU_BLOCKS):
        sl = slice(g * LRU_BLOCK, (g + 1) * LRU_BLOCK)
        gates = 0.5 + 0.5 * jnp.tanh(0.5 * pres[g])
        r = gates[:, :LRU_BLOCK]
        ig = gates[:, LRU_BLOCK:]
        log_a = lamc_ref[:, sl] * r
        a = jnp.exp(log_a)
        a_ref[:, sl] = a
        b_ref[:, sl] = jnp.sqrt(jnp.tanh(-log_a) * (1.0 + a * a)) * (ig * u[:, sl])

    def group(gi):
        g = (n_groups - 1 - gi) if reverse else gi
        return pl.ds(pl.multiple_of(g * SUBLANES, SUBLANES), SUBLANES)

    def scan(gi, state):
        h, p = state
        rows_g = group(gi)
        a = a_ref[rows_g, :]
        h = a * h + b_ref[rows_g, :]
        p = a * p
        a_ref[rows_g, :] = p
        b_ref[rows_g, :] = h
        return h, p

    zeros = jnp.zeros((SUBLANES, width), F32)
    h_end, p_end = lax.fori_loop(0, n_groups, scan, (zeros, zeros + 1.0), unroll=4)

    starts_seq = last if reverse else first
    incoming = spread(jnp.where(starts_seq, 0.0, carry_ref[0:1, :]))
    head = (SUBLANES - 1) if reverse else 0
    enter = incoming
    for _ in range(SUBLANES):
        leave = h_end + p_end * enter
        enter = jnp.where(rowi == head, incoming, pltpu.roll(leave, (SUBLANES - 1) if reverse else 1, 0))
    tail = 0 if reverse else SUBLANES - 1
    carry_ref[...] = spread((h_end + p_end * enter)[tail:tail + 1, :])

    def fix(gi, _):
        rows_g = group(gi)
        h_ref[rows_g, :] = b_ref[rows_g, :] + a_ref[rows_g, :] * enter
        return 0

    lax.fori_loop(0, n_groups, fix, 0, unroll=4)


def _lru_rev_kernel(xp_ref, xm_ref, xn_ref, cw_ref, cb_ref, wax_ref, bax_ref, lamc_ref,
                    out_ref, a_ref, b_ref, carry_ref, xs_ref, *, n_tiles, tiles_per_seq):
    t = (n_tiles - 1 - pl.program_id(0)) % tiles_per_seq
    _lru_tile(xp_ref, xm_ref, xn_ref, cw_ref, cb_ref, wax_ref, bax_ref, lamc_ref,
              a_ref, b_ref, carry_ref, xs_ref, out_ref, t=t, tiles_per_seq=tiles_per_seq, reverse=True)


def _lru_out_kernel(xp_ref, xm_ref, xn_ref, gate_ref, other_ref, x_ref, attn_ref,
                    cw_ref, cb_ref, wax_ref, bax_ref, lamc_ref, ga_ref, gl_ref, w_ref, unperm_ref, unclass_ref,
                    out_ref, a_ref, b_ref, carry_ref, xs_ref, h_ref, mixed_ref, stage_ref, *, n_tiles, tiles_per_seq):
    i = pl.program_id(0)

    @pl.when(i == 0)
    def _():
        mixed_ref[...] = jnp.zeros_like(mixed_ref)

    half_rows = IN_ROWS // MAX_DIL
    for r in range(MAX_DIL):
        stage_ref[r * half_rows:(r + 1) * half_rows, :] = _rms(attn_ref[r], ga_ref[...]).astype(BF16)
    attn_mixed = jnp.dot(unclass_ref[...], stage_ref[...], preferred_element_type=F32).astype(BF16)
    mixed = jnp.concatenate([attn_mixed, mixed_ref[...]], axis=1)
    half = out_ref.shape[1] // 2

    def project(cols):
        out_ref[:, cols] = x_ref[:, cols] + jnp.dot(mixed, w_ref[:, cols], preferred_element_type=F32)

    project(slice(0, half))
    t = jnp.minimum(i, n_tiles - 1) % tiles_per_seq
    _lru_tile(xp_ref, xm_ref, xn_ref, cw_ref, cb_ref, wax_ref, bax_ref, lamc_ref,
              a_ref, b_ref, carry_ref, xs_ref, h_ref, t=t, tiles_per_seq=tiles_per_seq, reverse=False,
              between=lambda: project(slice(half, 2 * half)))
    lru = (h_ref[...] + other_ref[...]) * _gelu_tanh(gate_ref[...])
    mixed_ref[...] = jnp.dot(unperm_ref[...], _rms(lru, gl_ref[...]).astype(BF16),
                             preferred_element_type=F32).astype(BF16)


def _lru_specs(rows, tile):
    tt = IN_ROWS
    before_rows = 2 * SUBLANES
    n_after_blocks = rows // SUBLANES
    x_specs = [
        pl.BlockSpec((before_rows, LRU_WIDTH), lambda i: (jnp.maximum(tile(i) * (tt // before_rows) - 1, 0), 0)),
        pl.BlockSpec((tt, LRU_WIDTH), lambda i: (tile(i), 0)),
        pl.BlockSpec((SUBLANES, LRU_WIDTH),
                     lambda i: (jnp.minimum((tile(i) + 1) * (tt // SUBLANES), n_after_blocks - 1), 0)),
    ]
    w_specs = [
        pl.BlockSpec((4, LRU_WIDTH), lambda i: (0, 0)),
        pl.BlockSpec((1, LRU_WIDTH), lambda i: (0, 0)),
        pl.BlockSpec((N_LRU_BLOCKS, LRU_BLOCK, 2 * LRU_BLOCK), lambda i: (0, 0, 0)),
        pl.BlockSpec((N_LRU_BLOCKS, 1, 2 * LRU_BLOCK), lambda i: (0, 0, 0)),
        pl.BlockSpec((1, LRU_WIDTH), lambda i: (0, 0)),
    ]
    scratch = [pltpu.VMEM((tt, LRU_WIDTH), F32), pltpu.VMEM((tt, LRU_WIDTH), F32),
               pltpu.VMEM((SUBLANES, LRU_WIDTH), F32), pltpu.VMEM((tt + 3 * SUBLANES, LRU_WIDTH), F32)]
    return x_specs, w_specs, scratch


def _lru_reverse(rg, conv_w, conv_b, wax, bax, lamc, seq):
    rows = rg.shape[0]
    tt = IN_ROWS
    n_tiles = rows // tt
    tile = lambda i: n_tiles - 1 - i
    x_specs, w_specs, scratch = _lru_specs(rows, tile)
    return pl.pallas_call(
        functools.partial(_lru_rev_kernel, n_tiles=n_tiles, tiles_per_seq=seq // tt),
        grid=(n_tiles,),
        in_specs=x_specs + w_specs,
        out_specs=pl.BlockSpec((tt, LRU_WIDTH), lambda i: (tile(i), 0)),
        out_shape=jax.ShapeDtypeStruct((rows, LRU_WIDTH), F32),
        scratch_shapes=scratch,
        compiler_params=_params(("arbitrary",)),
        name="lru_rev",
    )(rg, rg, rg, conv_w, conv_b, wax, bax, lamc)


def _lru_forward_outproj(x, rg, h_rev, attn, conv_w, conv_b, wax, bax, lamc, g_attn, g_lru, w_out, seq):
    rows, d = x.shape
    tt = IN_ROWS
    n_tiles = rows // tt
    tile = lambda i: jnp.minimum(i, n_tiles - 1)
    prev = lambda i: jnp.maximum(i - 1, 0)
    tiles_per_unit = UNIT // tt
    x_specs, w_specs, scratch = _lru_specs(rows, tile)
    x_specs += [
        pl.BlockSpec((tt, LRU_WIDTH), lambda i: (tile(i), 1)),
        pl.BlockSpec((tt, LRU_WIDTH), lambda i: (tile(i), 0)),
        pl.BlockSpec((tt, d), lambda i: (prev(i), 0)),
        pl.BlockSpec((None, MAX_DIL, tt // MAX_DIL, ATTN_WIDTH),
                     lambda i: (prev(i) // tiles_per_unit, 0, prev(i) % tiles_per_unit, 0)),
    ]
    w_specs += [
        pl.BlockSpec((1, ATTN_WIDTH), lambda i: (0, 0)),
        pl.BlockSpec((1, LRU_WIDTH), lambda i: (0, 0)),
        pl.BlockSpec(w_out.shape, lambda i: (0, 0), pipeline_mode=pl.Buffered(1)),
        pl.BlockSpec((tt, tt), lambda i: (0, 0)),
        pl.BlockSpec((tt, tt), lambda i: (0, 0)),
    ]
    scratch += [pltpu.VMEM((tt, LRU_WIDTH), F32), pltpu.VMEM((tt, LRU_WIDTH), BF16), pltpu.VMEM((tt, ATTN_WIDTH), BF16)]
    return pl.pallas_call(
        functools.partial(_lru_out_kernel, n_tiles=n_tiles, tiles_per_seq=seq // tt),
        grid=(n_tiles + 1,),
        in_specs=x_specs + w_specs,
        out_specs=pl.BlockSpec((tt, d), lambda i: (prev(i), 0)),
        out_shape=jax.ShapeDtypeStruct((rows, d), F32),
        scratch_shapes=scratch,
        compiler_params=_params(("arbitrary",)),
        name="lru_fwd_outproj",
    )(rg, rg, rg, rg, h_rev, x, attn.reshape(rows // UNIT, MAX_DIL, CLASS_ROWS, ATTN_WIDTH), conv_w, conv_b, wax, bax, lamc,
      g_attn, g_lru, w_out, _interleave_matrix().T, _class_major_matrix().T)


def _mlp_kernel(xp_ref, xm_ref, xn_ref, g_ref, wg_ref, wv_ref, cwg_ref, cwv_ref, cbg_ref, cbv_ref,
                wd_ref, gf_ref, out_ref, h_ref, *, tm, tiles_per_seq):
    i = pl.program_id(0)
    k = pl.program_id(1)
    t = i % tiles_per_seq
    n = tm + BF16_ROWS

    @pl.when(k == 0)
    def _():
        g = g_ref[...]
        h_ref[0:tm, :] = _rms(xm_ref[...], g).astype(BF16)
        after = jnp.where(t == tiles_per_seq - 1, 0.0, _rms(xn_ref[...], g))
        before = jnp.where(t == 0, 0.0, _rms(xp_ref[...], g))
        h_ref[tm:n, :] = jnp.concatenate([after, before], axis=0).astype(BF16)
        out_ref[...] = xm_ref[...]

    h = h_ref[...]

    def conv(w_ref, cw_ref, cb_ref):
        u = jnp.dot(h, w_ref[...], preferred_element_type=F32)
        c = (cb_ref[...] + pltpu.roll(u, 1, 0) * cw_ref[0:1, :] + u * cw_ref[1:2, :]
             + pltpu.roll(u, n - 1, 0) * cw_ref[2:3, :])
        return c[0:tm, :]

    act = (_gelu_tanh(conv(wg_ref, cwg_ref, cbg_ref)) * conv(wv_ref, cwv_ref, cbv_ref)).astype(BF16)
    out_ref[...] += jnp.dot(act, wd_ref[...], preferred_element_type=F32)

    @pl.when(k == pl.num_programs(1) - 1)
    def _():
        out_ref[...] = _rms(out_ref[...], gf_ref[...])


def _mlp(x, g_mlp, w_up, conv_w, conv_b, w_down, g_final, seq, tm, tf):
    rows, d = x.shape
    d_ff = w_down.shape[0]
    tm = min(tm, seq)
    tf = min(tf, d_ff)
    nk = d_ff // tf
    tiles_per_seq = seq // tm
    ratio = tm // SUBLANES
    n_halo_blocks = rows // SUBLANES
    return pl.pallas_call(
        functools.partial(_mlp_kernel, tm=tm, tiles_per_seq=tiles_per_seq),
        grid=(rows // tm, nk),
        in_specs=[
            pl.BlockSpec((SUBLANES, d), lambda i, k: (jnp.maximum(i * ratio - 1, 0), 0)),
            pl.BlockSpec((tm, d), lambda i, k: (i, 0)),
            pl.BlockSpec((SUBLANES, d), lambda i, k: (jnp.minimum((i + 1) * ratio, n_halo_blocks - 1), 0)),
            pl.BlockSpec((1, d), lambda i, k: (0, 0)),
            pl.BlockSpec((d, tf), lambda i, k: (0, k)),
            pl.BlockSpec((d, tf), lambda i, k: (0, nk + k)),
            pl.BlockSpec((3, tf), lambda i, k: (0, k)),
            pl.BlockSpec((3, tf), lambda i, k: (0, nk + k)),
            pl.BlockSpec((1, tf), lambda i, k: (0, k)),
            pl.BlockSpec((1, tf), lambda i, k: (0, nk + k)),
            pl.BlockSpec((tf, d), lambda i, k: (k, 0)),
            pl.BlockSpec((1, d), lambda i, k: (0, 0)),
        ],
        out_specs=pl.BlockSpec((tm, d), lambda i, k: (i, 0)),
        out_shape=jax.ShapeDtypeStruct((rows, d), F32),
        scratch_shapes=[pltpu.VMEM((tm + BF16_ROWS, d), BF16)],
        compiler_params=_params(("parallel", "arbitrary")),
        name="mlp",
    )(x, x, x, g_mlp, w_up, w_up, conv_w, conv_w, conv_b, conv_b, w_down, g_final)


def _prepare(g_mix, w_in, w_out, g_attn_out, g_lru_out, conv_rg_w, conv_rg_b, rg_w_a, rg_b_a,
             rg_w_x, rg_b_x, rg_lam, g_mlp, w_up, conv_ff_w, conv_ff_b, w_down, g_final):
    l = 0
    row = lambda v: v.reshape(1, -1).astype(F32)
    wax = jnp.concatenate([rg_w_a[l], rg_w_x[l]], axis=-1).astype(BF16)
    bax = jnp.concatenate([rg_b_a[l].reshape(2, N_LRU_BLOCKS, 1, LRU_BLOCK),
                           rg_b_x[l].reshape(2, N_LRU_BLOCKS, 1, LRU_BLOCK)], axis=-1).astype(F32)
    lamc = (-LRU_C * jax.nn.softplus(-rg_lam[l].astype(F32))).reshape(2, 1, LRU_WIDTH)
    return dict(
        g_mix=row(g_mix[l]), w_in=w_in[l].astype(BF16), w_out=w_out[l].astype(BF16),
        g_attn=row(g_attn_out[l]), g_lru=row(g_lru_out[l]),
        conv_rg_w=conv_rg_w[l].astype(F32), conv_rg_b=row(conv_rg_b[l]),
        wax=wax, bax=bax, lamc=lamc,
        g_mlp=row(g_mlp[l]), w_up=w_up[l].astype(BF16), conv_ff_w=conv_ff_w[l].astype(F32),
        conv_ff_b=row(conv_ff_b[l]), w_down=w_down[l].astype(BF16), g_final=row(g_final),
    )


def _encode(x, p, *, tm_mlp=512, tf=1024):
    batch, seq, d = x.shape
    x2 = x.reshape(batch * seq, d)
    nat, cm, rg = _inproj(x2, p["g_mix"], p["w_in"], seq)
    attn = _attention(nat, cm, seq)
    h_rev = _lru_reverse(rg, p["conv_rg_w"], p["conv_rg_b"], p["wax"][1], p["bax"][1], p["lamc"][1], seq)
    x1 = _lru_forward_outproj(x2, rg, h_rev, attn, p["conv_rg_w"], p["conv_rg_b"], p["wax"][0], p["bax"][0],
                              p["lamc"][0], p["g_attn"], p["g_lru"], p["w_out"], seq)
    y = _mlp(x1, p["g_mlp"], p["w_up"], p["conv_ff_w"], p["conv_ff_b"], p["w_down"], p["g_final"],
             seq, tm_mlp, tf)
    return y.reshape(batch, seq, d)


def kernel(x_prompt, x_sample, g_mix, w_in, w_out, g_attn_out, g_lru_out, conv_rg_w, conv_rg_b,
           rg_w_a, rg_b_a, rg_w_x, rg_b_x, rg_lam, g_mlp, w_up, conv_ff_w, conv_ff_b, w_down, g_final):
    p = _prepare(g_mix, w_in, w_out, g_attn_out, g_lru_out, conv_rg_w, conv_rg_b, rg_w_a, rg_b_a,
                 rg_w_x, rg_b_x, rg_lam, g_mlp, w_up, conv_ff_w, conv_ff_b, w_down, g_final)
    return (_encode(x_prompt, p), _encode(x_sample, p))
```

```python
import functools
import math

import jax
import jax.numpy as jnp
from jax import lax
from jax.experimental import pallas as pl
from jax.experimental.pallas import tpu as pltpu

F32 = jnp.float32
BF16 = jnp.bfloat16

HEAD_DIM = 128
N_HEADS = 8
ATTN_WIDTH = N_HEADS * HEAD_DIM
LRU_WIDTH = 1024
N_LRU_BLOCKS = 8
LRU_BLOCK = LRU_WIDTH // N_LRU_BLOCKS
IN_WIDTH = 3 * ATTN_WIDTH + 2 * LRU_WIDTH
MAX_DIL = 16
KEYS_PER_SIDE = 64
UNIT = MAX_DIL * KEYS_PER_SIDE
CLASS_ROWS = UNIT // MAX_DIL
TQ1 = 256
ATTN_SKEW = 6
ATTN_HEADS = 4
IN_ROWS = 512
IN_CHUNK = 256
IN_GROUP = 512
ROPE_THETA = 500000.0
ROPE_HALF = HEAD_DIM // 8
LRU_C = 8.0
NORM_EPS = 1e-6
NEG_INF = -1e30

SUBLANES = 8
BF16_ROWS = 16
VMEM_LIMIT = 60 * 1024 * 1024


def _params(semantics):
    return pltpu.CompilerParams(dimension_semantics=semantics, vmem_limit_bytes=VMEM_LIMIT)


def _rms(x, g):
    return x * lax.rsqrt(jnp.mean(x * x, axis=-1, keepdims=True) + NORM_EPS) * g


def _gelu_tanh(x):
    return x * (0.5 * (1.0 + jnp.tanh(math.sqrt(2.0 / math.pi) * (x + 0.044715 * (x * x * x)))))


def _inproj_kernel(x_ref, g_ref, w_ref, perm_ref, class_ref, cos_ref, sa_ref, sb_ref, nat_ref, cm_ref, rg_ref,
                   h_ref, hil_ref, zb_ref):
    h_ref[...] = _rms(x_ref[...], g_ref[...]).astype(BF16)
    hil_ref[...] = jnp.dot(perm_ref[...], h_ref[...], preferred_element_type=F32).astype(BF16)

    heads_per_chunk = IN_CHUNK // HEAD_DIM
    n_chunks = IN_WIDTH // IN_CHUNK
    n_qkv_chunks = 3 * ATTN_WIDTH // IN_CHUNK

    def project(c):
        cols = slice(c * IN_CHUNK, (c + 1) * IN_CHUNK)
        lhs = h_ref if c < n_qkv_chunks else hil_ref
        return jnp.dot(lhs[...], w_ref[:, cols], preferred_element_type=F32)

    half_rows = IN_ROWS // MAX_DIL
    scale = HEAD_DIM ** -0.5
    rotary = {0: (cos_ref[...] * scale, sa_ref[...] * scale, sb_ref[...] * scale),
              1: (cos_ref[...], sa_ref[...], sb_ref[...])}

    z_next = project(0)
    for c in range(n_chunks):
        z = z_next
        if c + 1 < n_chunks:
            z_next = project(c + 1)
        if c >= n_qkv_chunks:
            off = (c - n_qkv_chunks) * IN_CHUNK
            rg_ref[:, off:off + IN_CHUNK] = z
            continue
        for hc in range(heads_per_chunk):
            plane = c * heads_per_chunk + hc
            t = z[:, hc * HEAD_DIM:(hc + 1) * HEAD_DIM]
            if plane // N_HEADS in rotary:
                cos, sa, sb = rotary[plane // N_HEADS]
                t = t * cos + pltpu.roll(t, HEAD_DIM - ROPE_HALF, 1) * sa + pltpu.roll(t, ROPE_HALF, 1) * sb
            t = t.astype(BF16)
            zb_ref[:, plane * HEAD_DIM:(plane + 1) * HEAD_DIM] = t
            if plane >= N_HEADS:
                nat_ref[plane - N_HEADS] = t
        done = (c + 1) * IN_CHUNK
        if done % IN_GROUP == 0:
            cols = slice(done - IN_GROUP, done)
            grouped = jnp.dot(class_ref[...], zb_ref[:, cols], preferred_element_type=F32).astype(BF16)
            for hg in range(IN_GROUP // HEAD_DIM):
                plane = (done - IN_GROUP) // HEAD_DIM + hg
                for r in range(MAX_DIL):
                    cm_ref[plane, r] = grouped[r * half_rows:(r + 1) * half_rows, hg * HEAD_DIM:(hg + 1) * HEAD_DIM]


def _class_major_matrix():
    per_class = IN_ROWS // MAX_DIL
    row = jnp.arange(IN_ROWS)
    position = MAX_DIL * (row % per_class) + row // per_class
    return (position[:, None] == jnp.arange(IN_ROWS)[None, :]).astype(BF16)


def _interleave_matrix():
    n_groups = IN_ROWS // SUBLANES
    row = jnp.arange(IN_ROWS)
    time = n_groups * (row % SUBLANES) + row // SUBLANES
    return (time[:, None] == jnp.arange(IN_ROWS)[None, :]).astype(BF16)


def _rope_tables(seq):
    inv = ROPE_THETA ** (-jnp.arange(ROPE_HALF, dtype=F32) / ROPE_HALF)
    ang = jnp.arange(seq).astype(F32)[:, None] * inv[None, :]
    cos, sin = jnp.cos(ang), jnp.sin(ang)
    rest = HEAD_DIM - 2 * ROPE_HALF
    cos_t = jnp.concatenate([cos, cos, jnp.ones((seq, rest), F32)], axis=1)
    sa_t = jnp.concatenate([-sin, jnp.zeros((seq, HEAD_DIM - ROPE_HALF), F32)], axis=1)
    sb_t = jnp.concatenate([jnp.zeros((seq, ROPE_HALF), F32), sin, jnp.zeros((seq, rest), F32)], axis=1)
    return cos_t, sa_t, sb_t


def _inproj(x, g_mix, w_in, seq):
    rows, d = x.shape
    tiles_per_seq = seq // IN_ROWS
    tiles_per_unit = UNIT // IN_ROWS
    half_rows = IN_ROWS // MAX_DIL
    cos_t, sa_t, sb_t = _rope_tables(seq)
    tab = pl.BlockSpec((IN_ROWS, HEAD_DIM), lambda i: (i % tiles_per_seq, 0))
    n_planes = 3 * N_HEADS
    nat, cm, rg = pl.pallas_call(
        _inproj_kernel,
        grid=(rows // IN_ROWS,),
        in_specs=[
            pl.BlockSpec((IN_ROWS, d), lambda i: (i, 0)),
            pl.BlockSpec((1, d), lambda i: (0, 0)),
            pl.BlockSpec((d, IN_WIDTH), lambda i: (0, 0), pipeline_mode=pl.Buffered(1)),
            pl.BlockSpec((IN_ROWS, IN_ROWS), lambda i: (0, 0)),
            pl.BlockSpec((IN_ROWS, IN_ROWS), lambda i: (0, 0)),
            tab, tab, tab,
        ],
        out_specs=[
            pl.BlockSpec((2 * N_HEADS, IN_ROWS, HEAD_DIM), lambda i: (0, i, 0)),
            pl.BlockSpec((n_planes, None, MAX_DIL, half_rows, HEAD_DIM),
                         lambda i: (0, i // tiles_per_unit, 0, i % tiles_per_unit, 0)),
            pl.BlockSpec((IN_ROWS, 2 * LRU_WIDTH), lambda i: (i, 0)),
        ],
        out_shape=[
            jax.ShapeDtypeStruct((2 * N_HEADS, rows, HEAD_DIM), BF16),
            jax.ShapeDtypeStruct((n_planes, rows // UNIT, MAX_DIL, CLASS_ROWS, HEAD_DIM), BF16),
            jax.ShapeDtypeStruct((rows, 2 * LRU_WIDTH), F32),
        ],
        scratch_shapes=[pltpu.VMEM((IN_ROWS, d), BF16), pltpu.VMEM((IN_ROWS, d), BF16),
                        pltpu.VMEM((IN_ROWS, 3 * ATTN_WIDTH), BF16)],
        compiler_params=_params(("parallel",)),
        name="inproj",
    )(x, g_mix, w_in, _interleave_matrix(), _class_major_matrix(), cos_t, sa_t, sb_t)
    return nat, cm.reshape(n_planes, rows, HEAD_DIM), rg


def _attn_head_tiles(knp_ref, knm_ref, knn_ref, vnp_ref, vnm_ref, vnn_ref,
                     qc_ref, kcp_ref, kcm_ref, kcn_ref, vcp_ref, vcm_ref, vcn_ref,
                     out_ref, n1_ref, m1_ref, l1_ref, num_ref, max_ref, den_ref, *, has_prev, has_next):
    side = KEYS_PER_SIDE
    cat = lambda pieces: jnp.concatenate(pieces, axis=0)
    tiles = []

    n_tiles = UNIT // TQ1
    nk1 = TQ1 + 2 * side
    per_class = TQ1 // MAX_DIL
    row = lax.broadcasted_iota(jnp.int32, (TQ1, nk1), 0)
    col = lax.broadcasted_iota(jnp.int32, (TQ1, nk1), 1)
    q_pos = MAX_DIL * (row & (per_class - 1)) + (row >> 4)
    band = jnp.abs(col - side - q_pos) <= side

    def tile1(c):
        lo, hi = c * TQ1 - side, c * TQ1 + TQ1 + side
        pieces = [slice(r * CLASS_ROWS + c * per_class, r * CLASS_ROWS + (c + 1) * per_class) for r in range(MAX_DIL)]
        if c == 0:
            kv = lambda p, m, n: cat([p[...], m[0:hi, :]])
            mask = band & ((col >= side) | has_prev)
        elif c == n_tiles - 1:
            kv = lambda p, m, n: cat([m[lo:UNIT, :], n[...]])
            mask = band & ((col < TQ1 + side) | has_next)
        else:
            kv = lambda p, m, n: m[lo:hi, :]
            mask = band

        def finish(num, m, l):
            m = jnp.broadcast_to(m, (TQ1, HEAD_DIM))
            l = jnp.broadcast_to(l, (TQ1, HEAD_DIM))
            for r, rows_r in enumerate(pieces):
                got = slice(r * per_class, (r + 1) * per_class)
                n1_ref[rows_r, :] = num[got, :]
                m1_ref[rows_r, :] = m[got, :]
                l1_ref[rows_r, :] = l[got, :]

        return (lambda: cat([qc_ref[rows_r, :] for rows_r in pieces]), lambda: kv(knp_ref, knm_ref, knn_ref),
                lambda: kv(vnp_ref, vnm_ref, vnn_ref), mask, finish)

    tiles += [tile1(c) for c in range(n_tiles)]

    nk16 = 3 * CLASS_ROWS
    row = lax.broadcasted_iota(jnp.int32, (CLASS_ROWS, nk16), 0)
    col = lax.broadcasted_iota(jnp.int32, (CLASS_ROWS, nk16), 1)
    mask16 = ((jnp.abs(col - CLASS_ROWS - row) <= side) & ((col >= CLASS_ROWS) | has_prev)
              & ((col < 2 * CLASS_ROWS) | has_next))

    def tile16(r):
        sl = slice(r * CLASS_ROWS, (r + 1) * CLASS_ROWS)

        def finish(num, m, l):
            m1 = m1_ref[sl, :]
            m_new = jnp.maximum(m1, m)
            e1 = jnp.exp(m1 - m_new)
            e = jnp.exp(m - m_new)
            num_ref[sl, :] = e1 * n1_ref[sl, :] + e * num
            den_ref[sl, :] = e1 * l1_ref[sl, :] + e * l
            max_ref[sl, :] = m_new

        return (lambda: qc_ref[sl, :], lambda: cat([kcp_ref[sl, :], kcm_ref[sl, :], kcn_ref[sl, :]]),
                lambda: cat([vcp_ref[sl, :], vcm_ref[sl, :], vcn_ref[sl, :]]), mask16, finish)

    tiles += [tile16(r) for r in range(MAX_DIL)]

    edge = side // 4
    nq4 = 4 * CLASS_ROWS
    nk4 = nq4 + 2 * side
    row = lax.broadcasted_iota(jnp.int32, (nq4, nk4), 0)
    col = lax.broadcasted_iota(jnp.int32, (nq4, nk4), 1)
    q_idx = 4 * (row & (CLASS_ROWS - 1)) + (row >> 6)
    c_mid = col - side
    c_next = col - side - nq4
    k_idx = jnp.where(
        col < side, 4 * (CLASS_ROWS - edge + (col & (edge - 1))) + (col >> 4) - nq4,
        jnp.where(col < side + nq4, 4 * (c_mid & (CLASS_ROWS - 1)) + (c_mid >> 6),
                  4 * (c_next & (edge - 1)) + (c_next >> 4) + nq4))
    mask4 = ((jnp.abs(k_idx - q_idx) <= side) & ((col >= side) | has_prev)
             & ((col < side + nq4) | has_next))
    def tile4(r4):
        slabs = [slice((4 * s + r4) * CLASS_ROWS, (4 * s + r4 + 1) * CLASS_ROWS) for s in range(4)]

        def gather(ref, part):
            lo, hi = {"tail": (CLASS_ROWS - edge, CLASS_ROWS), "all": (0, CLASS_ROWS), "head": (0, edge)}[part]
            return [ref[sl.start + lo:sl.start + hi, :] for sl in slabs]

        def kv(p, m, n):
            return cat(gather(p, "tail") + gather(m, "all") + gather(n, "head"))

        def finish(num, m, l):
            m_old = cat([max_ref[sl, :] for sl in slabs])
            m_new = jnp.maximum(m_old, m)
            e_old = jnp.exp(m_old - m_new)
            e = jnp.exp(m - m_new)
            res = ((e_old * cat([num_ref[sl, :] for sl in slabs]) + e * num)
                   / (e_old * cat([den_ref[sl, :] for sl in slabs]) + e * l))
            for s, sl in enumerate(slabs):
                out_ref[sl, :] = res[CLASS_ROWS * s:CLASS_ROWS * (s + 1), :]

        return (lambda: cat(gather(qc_ref, "all")), lambda: kv(kcp_ref, kcm_ref, kcn_ref),
                lambda: kv(vcp_ref, vcm_ref, vcn_ref), mask4, finish)

    tiles += [tile4(r4) for r4 in range(4)]
    return tiles


def _attn_kernel(*refs, units_per_seq):
    n_in = 13
    ins, out_ref, scratch = refs[:n_in], refs[n_in], refs[n_in + 1:]
    u = pl.program_id(1) % units_per_seq
    tiles = []
    for hh in range(ATTN_HEADS):
        tiles += _attn_head_tiles(*[ref.at[hh] for ref in ins], out_ref.at[:, hh * HEAD_DIM:(hh + 1) * HEAD_DIM],
                                  *scratch[6 * hh:6 * (hh + 1)], has_prev=u > 0, has_next=u < units_per_seq - 1)

    def scores(tile):
        return lax.dot_general(tile[0](), tile[1](), (((1,), (1,)), ((), ())), preferred_element_type=F32)

    pending = [scores(t) for t in tiles[:ATTN_SKEW]]
    for idx, (_, _, v, mask, finish) in enumerate(tiles):
        s = pending.pop(0)
        if idx + ATTN_SKEW < len(tiles):
            pending.append(scores(tiles[idx + ATTN_SKEW]))
        s = jnp.where(mask, s, NEG_INF)
        m = jnp.max(s, axis=-1, keepdims=True)
        p = jnp.exp(s - m)
        l = jnp.sum(p, axis=-1, keepdims=True)
        finish(jnp.dot(p.astype(BF16), v(), preferred_element_type=F32), m, l)


def _attention(nat, cm, seq):
    rows = nat.shape[1]
    units_per_seq = seq // UNIT
    ratio = UNIT // KEYS_PER_SIDE
    n_side_blocks = rows // KEYS_PER_SIDE
    n_units = rows // UNIT

    blocks_per_plane = N_HEADS // ATTN_HEADS

    def unit(plane, shift):
        return pl.BlockSpec((ATTN_HEADS, UNIT, HEAD_DIM),
                            lambda h, u: (plane * blocks_per_plane + h, jnp.clip(u + shift, 0, n_units - 1), 0))

    def halo(plane, after):
        if after:
            index = lambda h, u: (plane * blocks_per_plane + h, jnp.minimum((u + 1) * ratio, n_side_blocks - 1), 0)
        else:
            index = lambda h, u: (plane * blocks_per_plane + h, jnp.maximum(u * ratio - 1, 0), 0)
        return pl.BlockSpec((ATTN_HEADS, KEYS_PER_SIDE, HEAD_DIM), index)

    nat_specs = [halo(0, False), unit(0, 0), halo(0, True), halo(1, False), unit(1, 0), halo(1, True)]
    cm_specs = [unit(0, 0), unit(1, -1), unit(1, 0), unit(1, 1), unit(2, -1), unit(2, 0), unit(2, 1)]
    return pl.pallas_call(
        functools.partial(_attn_kernel, units_per_seq=units_per_seq),
        grid=(blocks_per_plane, n_units),
        in_specs=nat_specs + cm_specs,
        out_specs=pl.BlockSpec((UNIT, ATTN_HEADS * HEAD_DIM), lambda h, u: (u, h)),
        out_shape=jax.ShapeDtypeStruct((rows, ATTN_WIDTH), F32),
        scratch_shapes=[pltpu.VMEM((UNIT, HEAD_DIM), F32)] * (6 * ATTN_HEADS),
        compiler_params=_params(("parallel", "parallel")),
        name="attention",
    )(*([nat] * 6), *([cm] * 7))


def _lru_tile(xp_ref, xm_ref, xn_ref, cw_ref, cb_ref, wax_ref, bax_ref, lamc_ref,
              a_ref, b_ref, carry_ref, xs_ref, h_ref, *, t, tiles_per_seq, reverse, between=None):
    tt = IN_ROWS
    n_groups = tt // SUBLANES
    width = a_ref.shape[1]
    rowi = lax.broadcasted_iota(jnp.int32, (SUBLANES, width), 0)
    first = t == 0
    last = t == tiles_per_seq - 1

    def spread(row):
        return jnp.broadcast_to(row, (SUBLANES, width))

    before = jnp.where(first, 0.0, xp_ref[...])
    after = jnp.where(last, 0.0, xn_ref[...])
    for k in (2, 1):
        own = pltpu.roll(xm_ref[tt - k * SUBLANES:tt - (k - 1) * SUBLANES, :], 1, 0)
        prev_last = before[(3 - k) * SUBLANES - 1:(3 - k) * SUBLANES, :]
        xs_ref[(2 - k) * SUBLANES:(3 - k) * SUBLANES, :] = jnp.where(rowi == 0, spread(prev_last), own)
    xs_ref[2 * SUBLANES:2 * SUBLANES + tt, :] = xm_ref[...]
    xs_ref[2 * SUBLANES + tt:3 * SUBLANES + tt, :] = jnp.where(
        rowi == SUBLANES - 1, spread(after[0:1, :]), pltpu.roll(xm_ref[0:SUBLANES, :], SUBLANES - 1, 0))
    u = cb_ref[...]
    for j in range(4):
        u = u + xs_ref[j * SUBLANES:j * SUBLANES + tt, :] * cw_ref[j:j + 1, :]
    ub = u.astype(BF16)

    pres = [jnp.dot(ub[:, g * LRU_BLOCK:(g + 1) * LRU_BLOCK], wax_ref[g], preferred_element_type=F32) + bax_ref[g]
            for g in range(N_LRU_BLOCKS)]
    if between is not None:
        between()

    for g in range(N_LRU_BLOCKS):
        sl = slice(g * LRU_BLOCK, (g + 1) * LRU_BLOCK)
        gates = 0.5 + 0.5 * jnp.tanh(0.5 * pres[g])
        r = gates[:, :LRU_BLOCK]
        ig = gates[:, LRU_BLOCK:]
        log_a = lamc_ref[:, sl] * r
        a = jnp.exp(log_a)
        a_ref[:, sl] = a
        y = jnp.tanh(-log_a) * (1.0 + a * a)
        b_ref[:, sl] = jnp.where(y > 0.0, y * lax.rsqrt(y), 0.0) * (ig * u[:, sl])

    def group(gi):
        g = (n_groups - 1 - gi) if reverse else gi
        return pl.ds(pl.multiple_of(g * SUBLANES, SUBLANES), SUBLANES)

    def scan(gi, state):
        h, p = state
        rows_g = group(gi)
        a = a_ref[rows_g, :]
        h = a * h + b_ref[rows_g, :]
        p = a * p
        a_ref[rows_g, :] = p
        b_ref[rows_g, :] = h
        return h, p

    zeros = jnp.zeros((SUBLANES, width), F32)
    h_end, p_end = lax.fori_loop(0, n_groups, scan, (zeros, zeros + 1.0), unroll=4)

    starts_seq = last if reverse else first
    incoming = spread(jnp.where(starts_seq, 0.0, carry_ref[0:1, :]))
    head = (SUBLANES - 1) if reverse else 0
    enter = incoming
    for _ in range(SUBLANES):
        leave = h_end + p_end * enter
        enter = jnp.where(rowi == head, incoming, pltpu.roll(leave, (SUBLANES - 1) if reverse else 1, 0))
    tail = 0 if reverse else SUBLANES - 1
    carry_ref[...] = spread((h_end + p_end * enter)[tail:tail + 1, :])

    def fix(gi, _):
        rows_g = group(gi)
        h_ref[rows_g, :] = b_ref[rows_g, :] + a_ref[rows_g, :] * enter
        return 0

    lax.fori_loop(0, n_groups, fix, 0, unroll=4)


def _lru_rev_kernel(xp_ref, xm_ref, xn_ref, cw_ref, cb_ref, wax_ref, bax_ref, lamc_ref,
                    out_ref, a_ref, b_ref, carry_ref, xs_ref, *, n_tiles, tiles_per_seq):
    t = (n_tiles - 1 - pl.program_id(0)) % tiles_per_seq
    _lru_tile(xp_ref, xm_ref, xn_ref, cw_ref, cb_ref, wax_ref, bax_ref, lamc_ref,
              a_ref, b_ref, carry_ref, xs_ref, out_ref, t=t, tiles_per_seq=tiles_per_seq, reverse=True)


def _lru_out_kernel(xp_ref, xm_ref, xn_ref, gate_ref, other_ref, x_ref, attn_ref,
                    cw_ref, cb_ref, wax_ref, bax_ref, lamc_ref, ga_ref, gl_ref, w_ref, unperm_ref, unclass_ref,
                    out_ref, a_ref, b_ref, carry_ref, xs_ref, h_ref, mixed_ref, stage_ref, *, n_tiles, tiles_per_seq):
    i = pl.program_id(0)

    @pl.when(i == 0)
    def _():
        mixed_ref[...] = jnp.zeros_like(mixed_ref)

    half_rows = IN_ROWS // MAX_DIL
    for r in range(MAX_DIL):
        stage_ref[r * half_rows:(r + 1) * half_rows, :] = _rms(attn_ref[r], ga_ref[...]).astype(BF16)
    attn_mixed = jnp.dot(unclass_ref[...], stage_ref[...], preferred_element_type=F32).astype(BF16)
    mixed = jnp.concatenate([attn_mixed, mixed_ref[...]], axis=1)
    half = out_ref.shape[1] // 2

    def project(cols):
        out_ref[:, cols] = x_ref[:, cols] + jnp.dot(mixed, w_ref[:, cols], preferred_element_type=F32)

    project(slice(0, half))
    t = jnp.minimum(i, n_tiles - 1) % tiles_per_seq
    _lru_tile(xp_ref, xm_ref, xn_ref, cw_ref, cb_ref, wax_ref, bax_ref, lamc_ref,
              a_ref, b_ref, carry_ref, xs_ref, h_ref, t=t, tiles_per_seq=tiles_per_seq, reverse=False,
              between=lambda: project(slice(half, 2 * half)))
    lru = (h_ref[...] + other_ref[...]) * _gelu_tanh(gate_ref[...])
    mixed_ref[...] = jnp.dot(unperm_ref[...], _rms(lru, gl_ref[...]).astype(BF16),
                             preferred_element_type=F32).astype(BF16)


def _lru_specs(rows, tile):
    tt = IN_ROWS
    before_rows = 2 * SUBLANES
    n_after_blocks = rows // SUBLANES
    x_specs = [
        pl.BlockSpec((before_rows, LRU_WIDTH), lambda i: (jnp.maximum(tile(i) * (tt // before_rows) - 1, 0), 0)),
        pl.BlockSpec((tt, LRU_WIDTH), lambda i: (tile(i), 0)),
        pl.BlockSpec((SUBLANES, LRU_WIDTH),
                     lambda i: (jnp.minimum((tile(i) + 1) * (tt // SUBLANES), n_after_blocks - 1), 0)),
    ]
    w_specs = [
        pl.BlockSpec((4, LRU_WIDTH), lambda i: (0, 0)),
        pl.BlockSpec((1, LRU_WIDTH), lambda i: (0, 0)),
        pl.BlockSpec((N_LRU_BLOCKS, LRU_BLOCK, 2 * LRU_BLOCK), lambda i: (0, 0, 0)),
        pl.BlockSpec((N_LRU_BLOCKS, 1, 2 * LRU_BLOCK), lambda i: (0, 0, 0)),
        pl.BlockSpec((1, LRU_WIDTH), lambda i: (0, 0)),
    ]
    scratch = [pltpu.VMEM((tt, LRU_WIDTH), F32), pltpu.VMEM((tt, LRU_WIDTH), F32),
               pltpu.VMEM((SUBLANES, LRU_WIDTH), F32), pltpu.VMEM((tt + 3 * SUBLANES, LRU_WIDTH), F32)]
    return x_specs, w_specs, scratch


def _lru_reverse(rg, conv_w, conv_b, wax, bax, lamc, seq):
    rows = rg.shape[0]
    tt = IN_ROWS
    n_tiles = rows // tt
    tile = lambda i: n_tiles - 1 - i
    x_specs, w_specs, scratch = _lru_specs(rows, tile)
    return pl.pallas_call(
        functools.partial(_lru_rev_kernel, n_tiles=n_tiles, tiles_per_seq=seq // tt),
        grid=(n_tiles,),
        in_specs=x_specs + w_specs,
        out_specs=pl.BlockSpec((tt, LRU_WIDTH), lambda i: (tile(i), 0)),
        out_shape=jax.ShapeDtypeStruct((rows, LRU_WIDTH), F32),
        scratch_shapes=scratch,
        compiler_params=_params(("arbitrary",)),
        name="lru_rev",
    )(rg, rg, rg, conv_w, conv_b, wax, bax, lamc)


def _lru_forward_outproj(x, rg, h_rev, attn, conv_w, conv_b, wax, bax, lamc, g_attn, g_lru, w_out, seq):
    rows, d = x.shape
    tt = IN_ROWS
    n_tiles = rows // tt
    tile = lambda i: jnp.minimum(i, n_tiles - 1)
    prev = lambda i: jnp.maximum(i - 1, 0)
    tiles_per_unit = UNIT // tt
    x_specs, w_specs, scratch = _lru_specs(rows, tile)
    x_specs += [
        pl.BlockSpec((tt, LRU_WIDTH), lambda i: (tile(i), 1)),
        pl.BlockSpec((tt, LRU_WIDTH), lambda i: (tile(i), 0)),
        pl.BlockSpec((tt, d), lambda i: (prev(i), 0)),
        pl.BlockSpec((None, MAX_DIL, tt // MAX_DIL, ATTN_WIDTH),
                     lambda i: (prev(i) // tiles_per_unit, 0, prev(i) % tiles_per_unit, 0)),
    ]
    w_specs += [
        pl.BlockSpec((1, ATTN_WIDTH), lambda i: (0, 0)),
        pl.BlockSpec((1, LRU_WIDTH), lambda i: (0, 0)),
        pl.BlockSpec(w_out.shape, lambda i: (0, 0), pipeline_mode=pl.Buffered(1)),
        pl.BlockSpec((tt, tt), lambda i: (0, 0)),
        pl.BlockSpec((tt, tt), lambda i: (0, 0)),
    ]
    scratch += [pltpu.VMEM((tt, LRU_WIDTH), F32), pltpu.VMEM((tt, LRU_WIDTH), BF16), pltpu.VMEM((tt, ATTN_WIDTH), BF16)]
    return pl.pallas_call(
        functools.partial(_lru_out_kernel, n_tiles=n_tiles, tiles_per_seq=seq // tt),
        grid=(n_tiles + 1,),
        in_specs=x_specs + w_specs,
        out_specs=pl.BlockSpec((tt, d), lambda i: (prev(i), 0)),
        out_shape=jax.ShapeDtypeStruct((rows, d), F32),
        scratch_shapes=scratch,
        compiler_params=_params(("arbitrary",)),
        name="lru_fwd_outproj",
    )(rg, rg, rg, rg, h_rev, x, attn.reshape(rows // UNIT, MAX_DIL, CLASS_ROWS, ATTN_WIDTH), conv_w, conv_b, wax, bax, lamc,
      g_attn, g_lru, w_out, _interleave_matrix().T, _class_major_matrix().T)


def _mlp_kernel(xp_ref, xm_ref, xn_ref, g_ref, wg_ref, wv_ref, cwg_ref, cwv_ref, cbg_ref, cbv_ref,
                wd_ref, gf_ref, out_ref, h_ref, *, tm, tiles_per_seq):
    i = pl.program_id(0)
    k = pl.program_id(1)
    t = i % tiles_per_seq
    n = tm + BF16_ROWS

    @pl.when(k == 0)
    def _():
        g = g_ref[...]
        h_ref[0:tm, :] = _rms(xm_ref[...], g).astype(BF16)
        after = jnp.where(t == tiles_per_seq - 1, 0.0, _rms(xn_ref[...], g))
        before = jnp.where(t == 0, 0.0, _rms(xp_ref[...], g))
        h_ref[tm:n, :] = jnp.concatenate([after, before], axis=0).astype(BF16)
        out_ref[...] = xm_ref[...]

    h = h_ref[...]

    def conv(w_ref, cw_ref, cb_ref):
        u = jnp.dot(h, w_ref[...], preferred_element_type=F32)
        c = (cb_ref[...] + pltpu.roll(u, 1, 0) * cw_ref[0:1, :] + u * cw_ref[1:2, :]
             + pltpu.roll(u, n - 1, 0) * cw_ref[2:3, :])
        return c[0:tm, :]

    act = (_gelu_tanh(conv(wg_ref, cwg_ref, cbg_ref)) * conv(wv_ref, cwv_ref, cbv_ref)).astype(BF16)
    out_ref[...] += jnp.dot(act, wd_ref[...], preferred_element_type=F32)

    @pl.when(k == pl.num_programs(1) - 1)
    def _():
        out_ref[...] = _rms(out_ref[...], gf_ref[...])


def _mlp(x, g_mlp, w_up, conv_w, conv_b, w_down, g_final, seq, tm, tf):
    rows, d = x.shape
    d_ff = w_down.shape[0]
    tm = min(tm, seq)
    tf = min(tf, d_ff)
    nk = d_ff // tf
    tiles_per_seq = seq // tm
    ratio = tm // SUBLANES
    n_halo_blocks = rows // SUBLANES
    return pl.pallas_call(
        functools.partial(_mlp_kernel, tm=tm, tiles_per_seq=tiles_per_seq),
        grid=(rows // tm, nk),
        in_specs=[
            pl.BlockSpec((SUBLANES, d), lambda i, k: (jnp.maximum(i * ratio - 1, 0), 0)),
            pl.BlockSpec((tm, d), lambda i, k: (i, 0)),
            pl.BlockSpec((SUBLANES, d), lambda i, k: (jnp.minimum((i + 1) * ratio, n_halo_blocks - 1), 0)),
            pl.BlockSpec((1, d), lambda i, k: (0, 0)),
            pl.BlockSpec((d, tf), lambda i, k: (0, k)),
            pl.BlockSpec((d, tf), lambda i, k: (0, nk + k)),
            pl.BlockSpec((3, tf), lambda i, k: (0, k)),
            pl.BlockSpec((3, tf), lambda i, k: (0, nk + k)),
            pl.BlockSpec((1, tf), lambda i, k: (0, k)),
            pl.BlockSpec((1, tf), lambda i, k: (0, nk + k)),
            pl.BlockSpec((tf, d), lambda i, k: (k, 0)),
            pl.BlockSpec((1, d), lambda i, k: (0, 0)),
        ],
        out_specs=pl.BlockSpec((tm, d), lambda i, k: (i, 0)),
        out_shape=jax.ShapeDtypeStruct((rows, d), F32),
        scratch_shapes=[pltpu.VMEM((tm + BF16_ROWS, d), BF16)],
        compiler_params=_params(("parallel", "arbitrary")),
        name="mlp",
    )(x, x, x, g_mlp, w_up, w_up, conv_w, conv_w, conv_b, conv_b, w_down, g_final)


def _prepare(g_mix, w_in, w_out, g_attn_out, g_lru_out, conv_rg_w, conv_rg_b, rg_w_a, rg_b_a,
             rg_w_x, rg_b_x, rg_lam, g_mlp, w_up, conv_ff_w, conv_ff_b, w_down, g_final):
    l = 0
    row = lambda v: v.reshape(1, -1).astype(F32)
    wax = jnp.concatenate([rg_w_a[l], rg_w_x[l]], axis=-1).astype(BF16)
    bax = jnp.concatenate([rg_b_a[l].reshape(2, N_LRU_BLOCKS, 1, LRU_BLOCK),
                           rg_b_x[l].reshape(2, N_LRU_BLOCKS, 1, LRU_BLOCK)], axis=-1).astype(F32)
    lamc = (-LRU_C * jax.nn.softplus(-rg_lam[l].astype(F32))).reshape(2, 1, LRU_WIDTH)
    return dict(
        g_mix=row(g_mix[l]), w_in=w_in[l].astype(BF16), w_out=w_out[l].astype(BF16),
        g_attn=row(g_attn_out[l]), g_lru=row(g_lru_out[l]),
        conv_rg_w=conv_rg_w[l].astype(F32), conv_rg_b=row(conv_rg_b[l]),
        wax=wax, bax=bax, lamc=lamc,
        g_mlp=row(g_mlp[l]), w_up=w_up[l].astype(BF16), conv_ff_w=conv_ff_w[l].astype(F32),
        conv_ff_b=row(conv_ff_b[l]), w_down=w_down[l].astype(BF16), g_final=row(g_final),
    )


def _encode(x, p, *, tm_mlp=512, tf=1024):
    batch, seq, d = x.shape
    x2 = x.reshape(batch * seq, d)
    nat, cm, rg = _inproj(x2, p["g_mix"], p["w_in"], seq)
    attn = _attention(nat, cm, seq)
    h_rev = _lru_reverse(rg, p["conv_rg_w"], p["conv_rg_b"], p["wax"][1], p["bax"][1], p["lamc"][1], seq)
    x1 = _lru_forward_outproj(x2, rg, h_rev, attn, p["conv_rg_w"], p["conv_rg_b"], p["wax"][0], p["bax"][0],
                              p["lamc"][0], p["g_attn"], p["g_lru"], p["w_out"], seq)
    y = _mlp(x1, p["g_mlp"], p["w_up"], p["conv_ff_w"], p["conv_ff_b"], p["w_down"], p["g_final"],
             seq, tm_mlp, tf)
    return y.reshape(batch, seq, d)


def kernel(x_prompt, x_sample, g_mix, w_in, w_out, g_attn_out, g_lru_out, conv_rg_w, conv_rg_b,
           rg_w_a, rg_b_a, rg_w_x, rg_b_x, rg_lam, g_mlp, w_up, conv_ff_w, conv_ff_b, w_down, g_final):
    p = _prepare(g_mix, w_in, w_out, g_attn_out, g_lru_out, conv_rg_w, conv_rg_b, rg_w_a, rg_b_a,
                 rg_w_x, rg_b_x, rg_lam, g_mlp, w_up, conv_ff_w, conv_ff_b, w_down, g_final)
    return (_encode(x_prompt, p), _encode(x_sample, p))
```
